```python
import jax, jax.numpy as jnp
from jax import lax
import numpy as np

D_MODEL = 2048
BATCH = 2
SEQ = 8192
DEPTH = 2

HEAD_DIM = 128
FOX_HEADS = 8
FOX_WIDTH = FOX_HEADS * HEAD_DIM
DSA_HEADS = 8
DSA_KV_HEADS = 2
DSA_WIDTH = DSA_HEADS * HEAD_DIM
DSA_KV_WIDTH = DSA_KV_HEADS * HEAD_DIM
IDX_HEADS = 16
IDX_DIM = 64
TOPK_MAX = 256
SSD_HEADS = 32
SSD_HEAD_DIM = 64
SSD_WIDTH = SSD_HEADS * SSD_HEAD_DIM
SSD_GROUPS = 4
SSD_STATE = 128
SSD_CHUNK = 128
CONV_WIDTH = 4
CONV_CH = SSD_WIDTH + 2 * SSD_GROUPS * SSD_STATE
N_BRANCH = 3
Q_BLOCK = 128
ROPE_THETA = 500000.0
ROPE_FRACTION = 4
EPS = 1e-6

SPLIT_SIZES = (
    FOX_WIDTH, FOX_WIDTH, FOX_WIDTH, FOX_HEADS, FOX_WIDTH,
    DSA_WIDTH, DSA_KV_WIDTH, DSA_KV_WIDTH,
    IDX_HEADS * IDX_DIM, IDX_DIM, IDX_HEADS, DSA_WIDTH,
    SSD_WIDTH, CONV_CH, SSD_HEADS,
    N_BRANCH * D_MODEL,
)
N_IN = sum(SPLIT_SIZES)

kernel_name = "fox_dsa_ssd_gated_hybrid"

F32 = jnp.float32


def rms_norm(x, w):
    xf = x.astype(F32)
    y = xf * lax.rsqrt(jnp.mean(xf * xf, axis=-1, keepdims=True) + EPS)
    return (y * w.astype(F32)).astype(x.dtype)


def partial_rope(x, positions):
    d = x.shape[-1]
    rd = d // ROPE_FRACTION
    half = rd // 2
    inv_freq = ROPE_THETA ** (-jnp.arange(half, dtype=F32) / half)
    ang = positions.astype(F32)[..., None] * inv_freq
    cos = jnp.cos(ang)[:, :, None, :]
    sin = jnp.sin(ang)[:, :, None, :]
    xr = x[..., :rd].astype(F32)
    x1, x2 = xr[..., :half], xr[..., half:]
    rot = jnp.concatenate([x1 * cos - x2 * sin, x2 * cos + x1 * sin], axis=-1)
    return jnp.concatenate([rot.astype(x.dtype), x[..., rd:]], axis=-1)


def fox_attention(q, k, v, log_f):
    B, S, H, d = q.shape
    nb = S // Q_BLOCK
    scale = d ** -0.5
    F = jnp.cumsum(log_f, axis=1)
    Fk = F.transpose(0, 2, 1)
    kpos = jnp.arange(S)
    qb = q.reshape(B, nb, Q_BLOCK, H, d).transpose(1, 0, 2, 3, 4)
    Fqb = F.reshape(B, nb, Q_BLOCK, H).transpose(1, 0, 3, 2)
    qposb = jnp.arange(S).reshape(nb, Q_BLOCK)

    def block(args):
        qi, Fq, qpos = args
        logits = jnp.einsum("bqhd,bkhd->bhqk", qi, k).astype(F32) * scale
        logits = logits + Fq[..., None] - Fk[:, :, None, :]
        logits = jnp.where(kpos[None, :] <= qpos[:, None], logits, -jnp.inf)
        p = jax.nn.softmax(logits, axis=-1).astype(v.dtype)
        return jnp.einsum("bhqk,bkhd->bqhd", p, v)

    out = lax.map(block, (qb, Fqb, qposb))
    return out.transpose(1, 0, 2, 3, 4).reshape(B, S, H, d)


def dsa_attention(q, k, v, q_idx, k_idx, w_idx):
    B, S, Hq, d = q.shape
    Hkv = k.shape[2]
    rep = Hq // Hkv
    topk = min(TOPK_MAX, S // 4)
    nb = S // Q_BLOCK
    scale = d ** -0.5
    idx_scale = (IDX_DIM ** -0.5) * (IDX_HEADS ** -0.5)
    kpos = jnp.arange(S)
    k_idx_f = k_idx.astype(F32)
    qb = q.reshape(B, nb, Q_BLOCK, Hkv, rep, d).transpose(1, 0, 2, 3, 4, 5)
    qib = q_idx.reshape(B, nb, Q_BLOCK, IDX_HEADS, IDX_DIM).transpose(1, 0, 2, 3, 4)
    wb = w_idx.reshape(B, nb, Q_BLOCK, IDX_HEADS).transpose(1, 0, 2, 3)
    qposb = jnp.arange(S).reshape(nb, Q_BLOCK)

    def block(args):
        qi, qI, wI, qpos = args
        s_h = jax.nn.relu(jnp.einsum("bqhe,bke->bqhk", qI.astype(F32), k_idx_f))
        score = jnp.einsum("bqh,bqhk->bqk", wI.astype(F32), s_h) * idx_scale
        score = jnp.where(kpos[None, None, :] <= qpos[None, :, None], score, -jnp.inf)
        _, sel = lax.top_k(score, topk)
        valid = sel <= qpos[None, :, None]
        flat = sel.reshape(B, Q_BLOCK * topk)[:, :, None, None]
        k_sel = jnp.take_along_axis(k, flat, axis=1).reshape(B, Q_BLOCK, topk, Hkv, d)
        v_sel = jnp.take_along_axis(v, flat, axis=1).reshape(B, Q_BLOCK, topk, Hkv, d)
        logits = jnp.einsum("bqgrd,bqkgd->bqgrk", qi, k_sel).astype(F32) * scale
        logits = jnp.where(valid[:, :, None, None, :], logits, -jnp.inf)
        p = jax.nn.softmax(logits, axis=-1).astype(v.dtype)
        return jnp.einsum("bqgrk,bqkgd->bqgrd", p, v_sel)

    out = lax.map(block, (qb, qib, wb, qposb))
    return out.transpose(1, 0, 2, 3, 4, 5).reshape(B, S, Hq, d)


def segsum(a):
    T = a.shape[-1]
    cs = jnp.cumsum(a, axis=-1)
    diff = cs[..., :, None] - cs[..., None, :]
    mask = jnp.tril(jnp.ones((T, T), dtype=bool))
    return jnp.where(mask, diff, -jnp.inf)


def ssd_scan(x, dt, a, b, c):
    Bsz, S, H, P = x.shape
    G, N = b.shape[2], b.shape[3]
    R = H // G
    L = SSD_CHUNK
    nc = S // L
    xd = (x.astype(F32) * dt[..., None]).reshape(Bsz, nc, L, G, R, P)
    da = (dt * a).reshape(Bsz, nc, L, G, R).transpose(0, 1, 3, 4, 2)
    bc = b.astype(F32).reshape(Bsz, nc, L, G, N)
    cc = c.astype(F32).reshape(Bsz, nc, L, G, N)
    a_cs = jnp.cumsum(da, axis=-1)
    decay_in = jnp.exp(segsum(da))
    cb = jnp.einsum("bclgn,bcsgn->bcgls", cc, bc)
    y_diag = jnp.einsum("bcgls,bcgrls,bcsgrp->bclgrp", cb, decay_in, xd)
    decay_to_end = jnp.exp(a_cs[..., -1:] - a_cs)
    chunk_states = jnp.einsum("bclgn,bcgrl,bclgrp->bcgrpn", bc, decay_to_end, xd)
    chunk_decay = jnp.exp(a_cs[..., -1])

    def step(state, inp):
        st_c, dec_c = inp
        return state * dec_c[..., None, None] + st_c, state

    init = jnp.zeros((Bsz, G, R, P, N), F32)
    _, prev = lax.scan(step, init, (chunk_states.transpose(1, 0, 2, 3, 4, 5),
                                    chunk_decay.transpose(1, 0, 2, 3)))
    prev = prev.transpose(1, 0, 2, 3, 4, 5)
    y_off = jnp.einsum("bclgn,bcgrpn,bcgrl->bclgrp", cc, prev, jnp.exp(a_cs))
    return (y_diag + y_off).reshape(Bsz, S, H, P)


def causal_depthwise_conv(x, w, b):
    K, C = w.shape
    out = lax.conv_general_dilated(
        x, w[:, None, :].astype(x.dtype), window_strides=(1,), padding=((K - 1, 0),),
        dimension_numbers=("NWC", "WIO", "NWC"), feature_group_count=C)
    return out + b.astype(x.dtype)


def hybrid_layer(x, c, positions, norm_w, w_ada, b_ada, w_in, b_fox_f, fox_q_norm, fox_k_norm,
                 dsa_q_norm, dsa_k_norm, conv_w, conv_b, dt_bias, a_log, d_skip, ssd_norm,
                 b_merge, w_o_fox, w_o_dsa, w_o_ssd, w_out):
    B, S, D = x.shape
    dtype = x.dtype
    mod = jnp.dot(jax.nn.silu(c), w_ada) + b_ada
    shift, scale, gate = jnp.split(mod, 3, axis=-1)
    u = rms_norm(x, norm_w) * (1.0 + scale[:, None, :]) + shift[:, None, :]

    proj = jnp.dot(u, w_in)
    split_points = [int(p) for p in np.cumsum(SPLIT_SIZES)[:-1]]
    (fq, fk, fv, ff, fg, dq, dk, dv, iq, ik, iw, dg,
     sz, sxbc, sdt, mg) = jnp.split(proj, split_points, axis=-1)

    fq = rms_norm(fq.reshape(B, S, FOX_HEADS, HEAD_DIM), fox_q_norm)
    fk = rms_norm(fk.reshape(B, S, FOX_HEADS, HEAD_DIM), fox_k_norm)
    fv = fv.reshape(B, S, FOX_HEADS, HEAD_DIM)
    log_f = jax.nn.log_sigmoid(ff.astype(F32) + b_fox_f.astype(F32))
    y_fox = fox_attention(fq, fk, fv, log_f).reshape(B, S, FOX_WIDTH) * jax.nn.silu(fg)

    dq = partial_rope(rms_norm(dq.reshape(B, S, DSA_HEADS, HEAD_DIM), dsa_q_norm), positions)
    dk = partial_rope(rms_norm(dk.reshape(B, S, DSA_KV_HEADS, HEAD_DIM), dsa_k_norm), positions)
    dv = dv.reshape(B, S, DSA_KV_HEADS, HEAD_DIM)
    iq = partial_rope(iq.reshape(B, S, IDX_HEADS, IDX_DIM), positions)
    ik = partial_rope(ik.reshape(B, S, 1, IDX_DIM), positions)[:, :, 0, :]
    y_dsa = dsa_attention(dq, dk, dv, iq, ik, iw).reshape(B, S, DSA_WIDTH) * jax.nn.silu(dg)

    xbc = jax.nn.silu(causal_depthwise_conv(sxbc, conv_w, conv_b))
    xs, bs, cs = jnp.split(xbc, [SSD_WIDTH, SSD_WIDTH + SSD_GROUPS * SSD_STATE], axis=-1)
    xs = xs.reshape(B, S, SSD_HEADS, SSD_HEAD_DIM)
    dt = jax.nn.softplus(sdt.astype(F32) + dt_bias.astype(F32))
    a = -jnp.exp(a_log.astype(F32))
    y = ssd_scan(xs, dt, a, bs.reshape(B, S, SSD_GROUPS, SSD_STATE),
                 cs.reshape(B, S, SSD_GROUPS, SSD_STATE))
    y = y + d_skip.astype(F32)[:, None] * xs.astype(F32)
    y = (y.reshape(B, S, SSD_WIDTH) * jax.nn.silu(sz.astype(F32))).astype(dtype)
    gsz = SSD_WIDTH // SSD_GROUPS
    y_ssd = rms_norm(y.reshape(B, S, SSD_GROUPS, gsz),
                     ssd_norm.reshape(SSD_GROUPS, gsz)).reshape(B, S, SSD_WIDTH)

    gates = jax.nn.sigmoid((mg.reshape(B, S, N_BRANCH, D) + b_merge).astype(F32)).astype(dtype)
    merged = (gates[:, :, 0] * jnp.dot(y_fox, w_o_fox)
              + gates[:, :, 1] * jnp.dot(y_dsa, w_o_dsa)
              + gates[:, :, 2] * jnp.dot(y_ssd, w_o_ssd))
    out = jnp.dot(merged, w_out)
    return x + gate[:, None, :] * out


def setup_inputs(seed: int = 0) -> dict:
    key = jax.random.key(seed)
    ks = jax.random.split(key, 24)

    def nrm(k, shape, s):
        return jax.random.normal(k, shape, F32) * s

    D = D_MODEL
    dt0 = jnp.exp(jax.random.uniform(ks[13], (DEPTH, SSD_HEADS), F32,
                                     minval=float(np.log(1e-3)), maxval=float(np.log(1e-1))))
    return {
        "x": nrm(ks[0], (BATCH, SEQ, D), 1.0),
        "c": nrm(ks[1], (BATCH, D), 1.0),
        "positions": jnp.tile(jnp.arange(SEQ, dtype=jnp.int32)[None, :], (BATCH, 1)),
        "norm_w": 1.0 + nrm(ks[2], (DEPTH, D), 0.02),
        "w_ada": nrm(ks[3], (DEPTH, D, 3 * D), 0.5 * D ** -0.5),
        "b_ada": nrm(ks[4], (DEPTH, 3 * D), 0.01),
        "w_in": nrm(ks[5], (DEPTH, D, N_IN), D ** -0.5),
        "b_fox_f": jax.random.uniform(ks[6], (DEPTH, FOX_HEADS), F32, minval=1.0, maxval=6.0),
        "fox_q_norm": 1.0 + nrm(ks[7], (DEPTH, HEAD_DIM), 0.02),
        "fox_k_norm": 1.0 + nrm(ks[8], (DEPTH, HEAD_DIM), 0.02),
        "dsa_q_norm": 1.0 + nrm(ks[9], (DEPTH, HEAD_DIM), 0.02),
        "dsa_k_norm": 1.0 + nrm(ks[10], (DEPTH, HEAD_DIM), 0.02),
        "conv_w": nrm(ks[11], (DEPTH, CONV_WIDTH, CONV_CH), CONV_WIDTH ** -0.5),
        "conv_b": nrm(ks[12], (DEPTH, CONV_CH), 0.01),
        "dt_bias": dt0 + jnp.log(-jnp.expm1(-dt0)),
        "a_log": jnp.log(jax.random.uniform(ks[14], (DEPTH, SSD_HEADS), F32, minval=1.0, maxval=16.0)),
        "d_skip": 1.0 + nrm(ks[15], (DEPTH, SSD_HEADS), 0.1),
        "ssd_norm": 1.0 + nrm(ks[16], (DEPTH, SSD_WIDTH), 0.02),
        "b_merge": nrm(ks[17], (DEPTH, N_BRANCH, D), 0.01),
        "w_o_fox": nrm(ks[18], (DEPTH, FOX_WIDTH, D), FOX_WIDTH ** -0.5),
        "w_o_dsa": nrm(ks[19], (DEPTH, DSA_WIDTH, D), DSA_WIDTH ** -0.5),
        "w_o_ssd": nrm(ks[20], (DEPTH, SSD_WIDTH, D), SSD_WIDTH ** -0.5),
        "w_out": nrm(ks[21], (DEPTH, D, D), D ** -0.5),
    }


def reference(x, c, positions, norm_w, w_ada, b_ada, w_in, b_fox_f, fox_q_norm, fox_k_norm,
              dsa_q_norm, dsa_k_norm, conv_w, conv_b, dt_bias, a_log, d_skip, ssd_norm,
              b_merge, w_o_fox, w_o_dsa, w_o_ssd, w_out):
    h = x
    for l in range(DEPTH):
        h = hybrid_layer(h, c, positions, norm_w[l], w_ada[l], b_ada[l], w_in[l], b_fox_f[l],
                         fox_q_norm[l], fox_k_norm[l], dsa_q_norm[l], dsa_k_norm[l],
                         conv_w[l], conv_b[l], dt_bias[l], a_log[l], d_skip[l], ssd_norm[l],
                         b_merge[l], w_o_fox[l], w_o_dsa[l], w_o_ssd[l], w_out[l])
    return h
```

```python
import functools

import numpy as np
import jax
import jax.numpy as jnp
from jax import lax
from jax.experimental import pallas as pl
from jax.experimental.pallas import tpu as pltpu

F32 = jnp.float32
BF16 = jnp.bfloat16
HIGHEST = lax.Precision.HIGHEST

HEAD_DIM = 128
FOX_HEADS = 8
FOX_WIDTH = FOX_HEADS * HEAD_DIM
DSA_HEADS = 8
DSA_KV_HEADS = 2
DSA_WIDTH = DSA_HEADS * HEAD_DIM
DSA_KV_WIDTH = DSA_KV_HEADS * HEAD_DIM
IDX_HEADS = 16
IDX_DIM = 64
TOPK_MAX = 256
SSD_HEADS = 32
SSD_HEAD_DIM = 64
SSD_WIDTH = SSD_HEADS * SSD_HEAD_DIM
SSD_GROUPS = 4
SSD_STATE = 128
SSD_CHUNK = 128
CONV_WIDTH = 4
CONV_CH = SSD_WIDTH + 2 * SSD_GROUPS * SSD_STATE
N_BRANCH = 3
ROPE_THETA = 500000.0
ROPE_FRACTION = 4
EPS = 1e-6

LANES = 128
NEG = -1e30
M_INIT = -1e29
INT_MIN = -(2 ** 31)
VMEM_LIMIT = 52 * 1024 * 1024

OFF_XBC = 0
OFF_FQ = OFF_XBC + CONV_CH
OFF_FK = OFF_FQ + FOX_WIDTH
OFF_FV = OFF_FK + FOX_WIDTH
OFF_FG = OFF_FV + FOX_WIDTH
OFF_DQ = OFF_FG + FOX_WIDTH
OFF_IQ = OFF_DQ + DSA_WIDTH
OFF_DG = OFF_IQ + IDX_HEADS * IDX_DIM
OFF_SZ = OFF_DG + DSA_WIDTH
OFF_MG = OFF_SZ + SSD_WIDTH
MISC_IK = 0
MISC_FF = MISC_IK + IDX_DIM
MISC_IW = MISC_FF + FOX_HEADS
MISC_DT = MISC_IW + IDX_HEADS
IN_TN = 768


def _cparams(*sem):
    return pltpu.CompilerParams(dimension_semantics=sem, vmem_limit_bytes=VMEM_LIMIT)


def _silu(x):
    return x * jax.nn.sigmoid(x)


def _softplus(x):
    return jnp.maximum(x, 0.0) + jnp.log1p(jnp.exp(-jnp.abs(x)))


def _ada_kernel(c_ref, w_ref, b_ref, o_ref):
    s = _silu(c_ref[...])
    o_ref[0] = jnp.dot(s, w_ref[0], precision=HIGHEST, preferred_element_type=F32) + b_ref[0]


def _ada(c8, w_ada, b_ada):
    depth, d, n = w_ada.shape
    tn = 512
    return pl.pallas_call(
        _ada_kernel,
        grid=(depth, n // tn),
        in_specs=[pl.BlockSpec((8, d), lambda l, j: (0, 0)),
                  pl.BlockSpec((1, d, tn), lambda l, j: (l, 0, j)),
                  pl.BlockSpec((1, 1, tn), lambda l, j: (l, 0, j))],
        out_specs=pl.BlockSpec((1, 8, tn), lambda l, j: (l, 0, j)),
        out_shape=jax.ShapeDtypeStruct((depth, 8, n), F32),
        compiler_params=_cparams("parallel", "parallel"),
        name="ada_mod",
    )(c8, w_ada, b_ada.reshape(depth, 1, n))


def _inproj_kernel(x_ref, nw_ref, sc_ref, sh_ref, w_ref, o_ref, u_ref, *, rows):
    @pl.when(pl.program_id(1) == 0)
    def _():
        nw = nw_ref[...]
        sc = 1.0 + sc_ref[...]
        sh = sh_ref[...]

        def body(r, carry):
            sl = pl.ds(pl.multiple_of(r * rows, rows), rows)
            x = x_ref[sl, :]
            ms = jnp.mean(x * x, axis=-1, keepdims=True)
            y = x * lax.rsqrt(ms + EPS) * nw
            u_ref[sl, :] = (y * sc + sh).astype(BF16)
            return carry

        lax.fori_loop(0, x_ref.shape[0] // rows, body, 0)

    o_ref[...] = jnp.dot(u_ref[...], w_ref[...], preferred_element_type=F32)


def _inproj(x2, norm_w, scale, shift, w_packed, seq):
    t, d = x2.shape
    n = w_packed.shape[1]
    tm = min(1024, seq)
    return pl.pallas_call(
        functools.partial(_inproj_kernel, rows=64),
        grid=(t // tm, n // IN_TN),
        in_specs=[pl.BlockSpec((tm, d), lambda i, j: (i, 0)),
                  pl.BlockSpec((1, d), lambda i, j: (0, 0)),
                  pl.BlockSpec((None, 1, d), lambda i, j: (i * tm // seq, 0, 0)),
                  pl.BlockSpec((None, 1, d), lambda i, j: (i * tm // seq, 0, 0)),
                  pl.BlockSpec((d, IN_TN), lambda i, j: (0, j))],
        out_specs=pl.BlockSpec((tm, IN_TN), lambda i, j: (i, j)),
        out_shape=jax.ShapeDtypeStruct((t, n), F32),
        scratch_shapes=[pltpu.VMEM((tm, d), BF16)],
        compiler_params=_cparams("parallel", "arbitrary"),
        name="norm_inproj",
    )(x2, norm_w.reshape(1, d), scale, shift, w_packed)


def _pack_w_in(w):
    sizes = (FOX_WIDTH, FOX_WIDTH, FOX_WIDTH, FOX_HEADS, FOX_WIDTH,
             DSA_WIDTH, DSA_KV_WIDTH, DSA_KV_WIDTH,
             IDX_HEADS * IDX_DIM, IDX_DIM, IDX_HEADS, DSA_WIDTH,
             SSD_WIDTH, CONV_CH, SSD_HEADS, N_BRANCH * w.shape[0])
    offs = np.concatenate([[0], np.cumsum(sizes)])
    (fq, fk, fv, ff, fg, dq, dk, dv, iq, ik, iw, dg, sz, sxbc, sdt, mg) = [
        w[:, int(offs[i]):int(offs[i + 1])].astype(BF16) for i in range(len(sizes))]
    d = w.shape[0]
    pad8 = jnp.zeros((d, LANES - (IDX_DIM + FOX_HEADS + IDX_HEADS + SSD_HEADS)), BF16)
    cols = [sxbc, fq, fk, fv, fg, dq, iq, dg, sz, mg, dk, dv, ik, ff, iw, sdt, pad8]
    packed = jnp.concatenate(cols, axis=1)
    pad = (-packed.shape[1]) % IN_TN
    return jnp.concatenate([packed, jnp.zeros((d, pad), BF16)], axis=1)


def _rope_tab_kernel(pos_ref, c128, sa128, sb128, c64, sa64, sb64, cik, saik, sbik):
    pos = pos_ref[...].astype(F32)
    lane = lax.broadcasted_iota(jnp.int32, (8, LANES), 1)[0:1, :]

    def tables(d):
        rd = d // ROPE_FRACTION
        half = rd // 2
        j = lane % d
        inv = jnp.power(ROPE_THETA, -(j % half).astype(F32) / half)
        ang = pos * inv
        cos = jnp.cos(ang)
        sin = jnp.sin(ang)
        c = jnp.where(j < rd, cos, 1.0)
        sa = jnp.where(j < half, -sin, 0.0)
        sb = jnp.where(j < half, 0.0, jnp.where(j < rd, sin, 0.0))
        return c, sa, sb

    c, sa, sb = tables(HEAD_DIM)
    c128[...] = c
    sa128[...] = sa
    sb128[...] = sb
    c, sa, sb = tables(IDX_DIM)
    c64[...] = c
    sa64[...] = sa
    sb64[...] = sb
    keep = lane < IDX_DIM
    cik[...] = jnp.where(keep, c, 0.0)
    saik[...] = jnp.where(keep, sa, 0.0)
    sbik[...] = jnp.where(keep, sb, 0.0)


def _rope_tables(pos2):
    t = pos2.shape[0]
    tp = min(512, t)
    spec = pl.BlockSpec((tp, LANES), lambda i: (i, 0))
    return pl.pallas_call(
        _rope_tab_kernel,
        grid=(t // tp,),
        in_specs=[pl.BlockSpec((tp, 1), lambda i: (i, 0))],
        out_specs=[spec] * 9,
        out_shape=[jax.ShapeDtypeStruct((t, LANES), F32)] * 9,
        compiler_params=_cparams("parallel"),
        name="rope_tables",
    )(pos2)


def _headprep_kernel(*refs, norm, rope_half, scale, dup_half, transpose):
    refs = list(refs)
    x = refs.pop(0)[...]
    if norm:
        w = refs.pop(0)[...]
        ms = jnp.mean(x * x, axis=-1, keepdims=True)
        x = x * lax.rsqrt(ms + EPS) * w
    if rope_half:
        c = refs.pop(0)[...]
        sa = refs.pop(0)[...]
        sb = refs.pop(0)[...]
        x = (x * c + pltpu.roll(x, LANES - rope_half, 1) * sa + pltpu.roll(x, rope_half, 1) * sb)
    if dup_half:
        x = x + pltpu.roll(x, LANES // 2, 1)
    if scale != 1.0:
        x = x * scale
    o_ref = refs.pop(0)
    if transpose:
        o_ref[...] = x.T.astype(BF16)
    else:
        o_ref[...] = x.astype(BF16)


def _headprep(proj, col_off, nblk, *, norm_w=None, tabs=None, rope_half=0, scale=1.0,
              dup_half=False, transpose=False, name="headprep"):
    t = proj.shape[0]
    tp = min(512, t)
    base = col_off // LANES
    ins = [proj]
    specs = [pl.BlockSpec((tp, LANES), lambda i, h: (i, base + h))]
    if norm_w is not None:
        ins.append(norm_w.reshape(1, LANES))
        specs.append(pl.BlockSpec((1, LANES), lambda i, h: (0, 0)))
    if tabs is not None:
        ins.extend(tabs)
        specs.extend([pl.BlockSpec((tp, LANES), lambda i, h: (i, 0))] * 3)
    if transpose:
        out_shape = jax.ShapeDtypeStruct((nblk * LANES, t), BF16)
        out_spec = pl.BlockSpec((LANES, tp), lambda i, h: (h, i))
    else:
        out_shape = jax.ShapeDtypeStruct((t, nblk * LANES), BF16)
        out_spec = pl.BlockSpec((tp, LANES), lambda i, h: (i, h))
    kern = functools.partial(_headprep_kernel, norm=norm_w is not None, rope_half=rope_half,
                             scale=scale, dup_half=dup_half, transpose=transpose)
    return pl.pallas_call(
        kern, grid=(t // tp, nblk), in_specs=specs, out_specs=out_spec, out_shape=out_shape,
        compiler_params=_cparams("parallel", "arbitrary"), name=name,
    )(*ins)


def _fox_f_kernel(misc_ref, bias_ref, fcol_ref, frow_ref, carry_ref, *, tiles_per_batch):
    @pl.when(pl.program_id(0) % tiles_per_batch == 0)
    def _():
        carry_ref[...] = jnp.zeros_like(carry_ref)

    z = misc_ref[...] + bias_ref[...]
    lf = jnp.minimum(z, 0.0) - jnp.log1p(jnp.exp(-jnp.abs(z)))
    tp = z.shape[0]
    r = lax.broadcasted_iota(jnp.int32, (tp, tp), 0)
    c = lax.broadcasted_iota(jnp.int32, (tp, tp), 1)
    tril = jnp.where(c <= r, 1.0, 0.0)
    cs = jnp.dot(tril, lf, precision=HIGHEST, preferred_element_type=F32) + carry_ref[0:1, :]
    carry_ref[...] = jnp.broadcast_to(cs[tp - 1:tp, :], carry_ref.shape)
    fcol_ref[...] = cs
    frow_ref[...] = cs.T[MISC_FF:MISC_FF + FOX_HEADS, :]


def _fox_f_call(proj, bias128, seq, misc_blk):
    t = proj.shape[0]
    tp = min(512, seq)
    return pl.pallas_call(
        functools.partial(_fox_f_kernel, tiles_per_batch=seq // tp),
        grid=(t // tp,),
        in_specs=[pl.BlockSpec((tp, LANES), lambda i: (i, misc_blk)),
                  pl.BlockSpec((1, LANES), lambda i: (0, 0))],
        out_specs=[pl.BlockSpec((tp, LANES), lambda i: (i, 0)),
                   pl.BlockSpec((FOX_HEADS, tp), lambda i: (0, i))],
        out_shape=[jax.ShapeDtypeStruct((t, LANES), F32),
                   jax.ShapeDtypeStruct((FOX_HEADS, t), F32)],
        scratch_shapes=[pltpu.VMEM((8, LANES), F32)],
        compiler_params=_cparams("arbitrary"),
        name="fox_logf_cumsum",
    )(proj, bias128)


def _fox_kernel(q_ref, kt_ref, v_ref, fcol_ref, frow_ref, g_ref, o_ref, *, tq, tk):
    h = pl.program_id(1)
    i = pl.program_id(2)
    q = q_ref[...]
    lane = lax.broadcasted_iota(jnp.int32, (tq, LANES), 1)
    fq = jnp.sum(jnp.where(lane == MISC_FF + h, fcol_ref[...], 0.0), axis=-1, keepdims=True)
    row = lax.broadcasted_iota(jnp.int32, (tq, tk), 0)
    col = lax.broadcasted_iota(jnp.int32, (tq, tk), 1)

    def step(j, carry, masked):
        m, l, acc = carry
        start = pl.multiple_of(j * tk, tk)
        s = jnp.dot(q, kt_ref[:, pl.ds(start, tk)], preferred_element_type=F32)
        s = s + (fq - frow_ref[:, pl.ds(start, tk)])
        if masked:
            s = jnp.where(col <= row, s, NEG)
        m_new = jnp.maximum(m, jnp.max(s, axis=-1, keepdims=True))
        alpha = jnp.exp(m - m_new)
        p = jnp.exp(s - m_new)
        l = alpha * l + jnp.sum(p, axis=-1, keepdims=True)
        acc = alpha * acc + jnp.dot(p.astype(BF16), v_ref[pl.ds(start, tk), :],
                                    preferred_element_type=F32)
        return m_new, l, acc

    init = (jnp.full((tq, 1), NEG, F32), jnp.zeros((tq, 1), F32), jnp.zeros((tq, LANES), F32))
    carry = lax.fori_loop(0, i, lambda j, c: step(j, c, False), init)
    m, l, acc = step(i, carry, True)
    g = g_ref[...]
    o_ref[...] = ((acc / l) * _silu(g)).astype(BF16)


def _fox_attention(qn, kt, v, fcol, frow3, proj, batch, seq):
    t = qn.shape[0]
    tq = tk = min(512, seq)
    nq = seq // tq
    gblk = OFF_FG // LANES
    return pl.pallas_call(
        functools.partial(_fox_kernel, tq=tq, tk=tk),
        grid=(batch, FOX_HEADS, nq),
        in_specs=[pl.BlockSpec((tq, LANES), lambda b, h, i: (b * nq + i, h)),
                  pl.BlockSpec((LANES, seq), lambda b, h, i: (h, b)),
                  pl.BlockSpec((seq, LANES), lambda b, h, i: (b, h)),
                  pl.BlockSpec((tq, LANES), lambda b, h, i: (b * nq + i, 0)),
                  pl.BlockSpec((None, 1, seq), lambda b, h, i: (h, 0, b)),
                  pl.BlockSpec((tq, LANES), lambda b, h, i: (b * nq + i, gblk + h))],
        out_specs=pl.BlockSpec((tq, LANES), lambda b, h, i: (b * nq + i, h)),
        out_shape=jax.ShapeDtypeStruct((t, FOX_WIDTH), BF16),
        compiler_params=_cparams("parallel", "parallel", "arbitrary"),
        name="fox_attention",
    )(qn, kt, v, fcol, frow3, proj)


def _dsa_kernel(dq_ref, kt_ref, v_ref, iq_ref, kkt_ref, misc_ref, g_ref, o_ref,
                keys_ref, qm_ref, *, tq, tk, topk):
    i = pl.program_id(1)
    nkt = (i * tq) // tk + 1
    q0 = i * tq
    row = q0 + lax.broadcasted_iota(jnp.int32, (tq, tk), 0)
    col0 = lax.broadcasted_iota(jnp.int32, (tq, tk), 1)
    lane = lax.broadcasted_iota(jnp.int32, (tq, LANES), 1)
    idx_scale = (IDX_DIM ** -0.5) * (IDX_HEADS ** -0.5)
    w_all = misc_ref[...] * idx_scale

    zero = jnp.zeros((tq, LANES), BF16)
    for p in range(IDX_HEADS // 2):
        blk = iq_ref[:, p * LANES:(p + 1) * LANES]
        qm_ref[2 * p] = jnp.where(lane < IDX_DIM, blk, zero)
        qm_ref[2 * p + 1] = jnp.where(lane < IDX_DIM, zero, blk)

    def score_tile(j, carry):
        start = pl.multiple_of(j * tk, tk)
        kk = kkt_ref[:, pl.ds(start, tk)]
        acc = jnp.zeros((tq, tk), F32)
        for hh in range(IDX_HEADS):
            s = jnp.dot(qm_ref[hh], kk, preferred_element_type=F32)
            wh = w_all[:, MISC_IW + hh:MISC_IW + hh + 1]
            acc = acc + wh * jnp.maximum(s, 0.0)
        bits = pltpu.bitcast(acc, jnp.int32)
        key = bits ^ ((bits >> 31) & 0x7FFFFFFF)
        keys_ref[:, pl.ds(start, tk)] = jnp.where(start + col0 <= row, key, INT_MIN)
        return carry

    lax.fori_loop(0, nkt, score_tile, 0)

    def count_ge(cand):
        def body(j, c):
            start = pl.multiple_of(j * tk, tk)
            ge = jnp.where(keys_ref[:, pl.ds(start, tk)] >= cand, 1.0, 0.0)
            part = ge[:, 0:LANES]
            for u in range(1, tk // LANES):
                part = part + ge[:, u * LANES:(u + 1) * LANES]
            return c + part
        c = lax.fori_loop(0, nkt, body, jnp.zeros((tq, LANES), F32))
        return jnp.sum(c, axis=-1, keepdims=True)

    def search(p, thr):
        cand = thr + jnp.left_shift(jnp.int32(1), 31 - p)
        return jnp.where(count_ge(cand) >= topk, cand, thr)

    thr = lax.fori_loop(0, 32, search, jnp.full((tq, 1), INT_MIN, jnp.int32))

    cnt_ge = count_ge(thr)
    cnt_gt = count_ge(thr + 1)
    tied = jnp.where((cnt_ge > topk) & (thr > INT_MIN), 1.0, 0.0)

    @pl.when(jnp.max(tied) > 0.0)
    def _():
        quota = topk - cnt_gt
        ur = lax.broadcasted_iota(jnp.int32, (tk, tk), 0)
        uc = lax.broadcasted_iota(jnp.int32, (tk, tk), 1)
        triu = jnp.where(ur <= uc, 1.0, 0.0).astype(BF16)

        def fix(j, seen):
            start = pl.multiple_of(j * tk, tk)
            k = keys_ref[:, pl.ds(start, tk)]
            eq = jnp.where(k == thr, 1.0, 0.0)
            rank = seen + jnp.dot(eq.astype(BF16), triu, preferred_element_type=F32)
            drop = (eq * tied) * jnp.where(rank > quota, 1.0, 0.0)
            keys_ref[:, pl.ds(start, tk)] = jnp.where(drop > 0.0, thr - 1, k)
            return seen + jnp.sum(eq, axis=-1, keepdims=True)

        lax.fori_loop(0, nkt, fix, jnp.zeros((tq, 1), F32))

    thr = jnp.maximum(thr, INT_MIN + 1)

    rep = DSA_HEADS // DSA_KV_HEADS
    qs = [jnp.concatenate([dq_ref[:, (g * rep + r) * LANES:(g * rep + r + 1) * LANES]
                           for r in range(rep)], axis=0) for g in range(DSA_KV_HEADS)]

    def attend(j, carry):
        start = pl.multiple_of(j * tk, tk)
        nb = jnp.where(keys_ref[:, pl.ds(start, tk)] >= thr, 0.0, NEG)
        nb = jnp.concatenate([nb] * rep, axis=0)
        out = []
        for g in range(DSA_KV_HEADS):
            m, l, acc = carry[3 * g:3 * g + 3]
            s = jnp.dot(qs[g], kt_ref[g * LANES:(g + 1) * LANES, pl.ds(start, tk)],
                        preferred_element_type=F32) + nb
            m_new = jnp.maximum(m, jnp.max(s, axis=-1, keepdims=True))
            alpha = jnp.exp(m - m_new)
            p = jnp.exp(s - m_new)
            l = alpha * l + jnp.sum(p, axis=-1, keepdims=True)
            acc = alpha * acc + jnp.dot(p.astype(BF16),
                                        v_ref[pl.ds(start, tk), g * LANES:(g + 1) * LANES],
                                        preferred_element_type=F32)
            out.extend([m_new, l, acc])
        return tuple(out)

    init = (jnp.full((rep * tq, 1), M_INIT, F32), jnp.zeros((rep * tq, 1), F32),
            jnp.zeros((rep * tq, LANES), F32)) * DSA_KV_HEADS
    res = lax.fori_loop(0, nkt, attend, init)
    for g in range(DSA_KV_HEADS):
        o = res[3 * g + 2] / res[3 * g + 1]
        for r in range(rep):
            sl = slice((g * rep + r) * LANES, (g * rep + r + 1) * LANES)
            o_ref[:, sl] = (o[r * tq:(r + 1) * tq, :] * _silu(g_ref[:, sl])).astype(BF16)


def _dsa_attention(dq, dkt, dv, iq, kkt, proj, batch, seq, misc_blk):
    t = dq.shape[0]
    tq = 128
    tk = min(512, seq)
    nq = seq // tq
    topk = min(TOPK_MAX, seq // 4)
    return pl.pallas_call(
        functools.partial(_dsa_kernel, tq=tq, tk=tk, topk=topk),
        grid=(batch, nq),
        in_specs=[pl.BlockSpec((tq, DSA_WIDTH), lambda b, i: (b * nq + i, 0)),
                  pl.BlockSpec((DSA_KV_WIDTH, seq), lambda b, i: (0, b)),
                  pl.BlockSpec((seq, DSA_KV_WIDTH), lambda b, i: (b, 0)),
                  pl.BlockSpec((tq, IDX_HEADS * IDX_DIM), lambda b, i: (b * nq + i, 0)),
                  pl.BlockSpec((LANES, seq), lambda b, i: (0, b)),
                  pl.BlockSpec((tq, LANES), lambda b, i: (b * nq + i, misc_blk)),
                  pl.BlockSpec((tq, DSA_WIDTH), lambda b, i: (b * nq + i, OFF_DG // DSA_WIDTH))],
        out_specs=pl.BlockSpec((tq, DSA_WIDTH), lambda b, i: (b * nq + i, 0)),
        out_shape=jax.ShapeDtypeStruct((t, DSA_WIDTH), BF16),
        scratch_shapes=[pltpu.VMEM((tq, seq), jnp.int32),
                        pltpu.VMEM((IDX_HEADS, tq, LANES), BF16)],
        compiler_params=_cparams("parallel", "arbitrary"),
        name="dsa_attention",
    )(dq, dkt, dv, iq, kkt, proj, proj)


def _ssd_kernel(xin_ref, z_ref, misc_ref, cw_ref, cb_ref, dtb_ref, alog_ref, dsk_ref, nw_ref,
                e_ref, o_ref, xbuf, xbc, ybuf, state, *, L):
    @pl.when(pl.program_id(1) == 0)
    def _():
        xbuf[0:8, :] = jnp.zeros((8, CONV_CH), F32)
        state[...] = jnp.zeros_like(state)

    xbuf[8:8 + L, :] = xin_ref[...]
    lc = 512
    for ch in range(CONV_CH // lc):
        ls = slice(ch * lc, (ch + 1) * lc)
        acc = cb_ref[:, ls] + cw_ref[3:4, ls] * xbuf[8:8 + L, ls]
        acc = acc + cw_ref[2:3, ls] * xbuf[7:7 + L, ls]
        acc = acc + cw_ref[1:2, ls] * xbuf[6:6 + L, ls]
        acc = acc + cw_ref[0:1, ls] * xbuf[5:5 + L, ls]
        xbc[:, ls] = _silu(acc)
    xbuf[0:8, :] = xbuf[L:L + 8, :]

    dt = _softplus(misc_ref[...] + dtb_ref[...])
    a = -jnp.exp(alog_ref[...])
    da = dt * a
    r = lax.broadcasted_iota(jnp.int32, (L, L), 0)
    c = lax.broadcasted_iota(jnp.int32, (L, L), 1)
    causal = c <= r
    tril = jnp.where(causal, 1.0, 0.0)
    acs = jnp.dot(tril, da, precision=HIGHEST, preferred_element_type=F32)
    ea = jnp.exp(acs)
    dte = jnp.exp(acs[L - 1:L, :] - acs)
    full = jnp.dot(jnp.concatenate([dt, ea, dte], axis=0), e_ref[...],
                   precision=HIGHEST, preferred_element_type=F32)
    dt_f = full[0:L]
    ea_f = full[L:2 * L]
    dte_f = full[2 * L:3 * L]
    acs_t = acs.T
    lane = lax.broadcasted_iota(jnp.int32, (L, LANES), 1)

    gw = SSD_WIDTH // SSD_GROUPS
    hpg = SSD_HEADS // SSD_GROUPS
    for g in range(SSD_GROUPS):
        gs = slice(g * gw, (g + 1) * gw)
        xd = xbc[:, gs] * dt_f[:, gs]
        xd_b = xd.astype(BF16)
        bg = xbc[:, SSD_WIDTH + g * SSD_STATE:SSD_WIDTH + (g + 1) * SSD_STATE]
        cg = xbc[:, SSD_WIDTH + (SSD_GROUPS + g) * SSD_STATE:
                 SSD_WIDTH + (SSD_GROUPS + g + 1) * SSD_STATE].astype(BF16)
        cbm = lax.dot_general(cg, bg.astype(BF16), (((1,), (1,)), ((), ())),
                              preferred_element_type=F32)
        for pr in range(hpg // 2):
            xpair = xd_b[:, pr * LANES:(pr + 1) * LANES]
            ys = []
            for hh in range(2):
                hl = MISC_DT + g * hpg + 2 * pr + hh
                seg = jnp.where(causal, acs[:, hl:hl + 1] - acs_t[hl:hl + 1, :], NEG)
                mm = (cbm * jnp.exp(seg)).astype(BF16)
                ys.append(jnp.dot(mm, xpair, preferred_element_type=F32))
            ybuf[:, g * gw + pr * LANES:g * gw + (pr + 1) * LANES] = jnp.where(
                lane < SSD_HEAD_DIM, ys[0], ys[1])
        st = state[:, gs]
        yoff = jnp.dot(cg, st.astype(BF16), preferred_element_type=F32)
        ybuf[:, gs] = ybuf[:, gs] + yoff * ea_f[:, gs]
        zz = (xd * dte_f[:, gs]).astype(BF16)
        state[:, gs] = st * ea_f[L - 1:L, gs] + jnp.dot(bg.T.astype(BF16), zz,
                                                        preferred_element_type=F32)
        y = ybuf[:, gs] + dsk_ref[:, gs] * xbc[:, gs]
        y = y * _silu(z_ref[:, gs])
        ms = jnp.mean(y * y, axis=-1, keepdims=True)
        o_ref[:, gs] = (y * lax.rsqrt(ms + EPS) * nw_ref[:, gs]).astype(BF16)


def _ssd(proj, conv_w, conv_b, dtb128, alog128, dskip_f, ssd_norm, expand, batch, seq, misc_blk):
    t = proj.shape[0]
    L = SSD_CHUNK
    nc = seq // L
    const = lambda shape: pl.BlockSpec(shape, lambda b, c: (0, 0))
    return pl.pallas_call(
        functools.partial(_ssd_kernel, L=L),
        grid=(batch, nc),
        in_specs=[pl.BlockSpec((L, CONV_CH), lambda b, c: (b * nc + c, 0)),
                  pl.BlockSpec((L, SSD_WIDTH), lambda b, c: (b * nc + c, OFF_SZ // SSD_WIDTH)),
                  pl.BlockSpec((L, LANES), lambda b, c: (b * nc + c, misc_blk)),
                  const((CONV_WIDTH, CONV_CH)), const((1, CONV_CH)), const((1, LANES)),
                  const((1, LANES)), const((1, SSD_WIDTH)), const((1, SSD_WIDTH)),
                  const((LANES, SSD_WIDTH))],
        out_specs=pl.BlockSpec((L, SSD_WIDTH), lambda b, c: (b * nc + c, 0)),
        out_shape=jax.ShapeDtypeStruct((t, SSD_WIDTH), BF16),
        scratch_shapes=[pltpu.VMEM((L + 8, CONV_CH), F32), pltpu.VMEM((L, CONV_CH), F32),
                        pltpu.VMEM((L, SSD_WIDTH), F32), pltpu.VMEM((SSD_STATE, SSD_WIDTH), F32)],
        compiler_params=_cparams("parallel", "arbitrary"),
        name="ssd_scan",
    )(proj, proj, proj, conv_w, conv_b.reshape(1, CONV_CH), dtb128, alog128, dskip_f,
      ssd_norm.reshape(1, SSD_WIDTH), expand)


def _merge_kernel(yf_ref, yd_ref, ys_ref, wf_ref, wd_ref, ws_ref, g0_ref, g1_ref, g2_ref,
                  bm_ref, o_ref):
    a = jnp.dot(yf_ref[...], wf_ref[...], preferred_element_type=F32)
    b = jnp.dot(yd_ref[...], wd_ref[...], preferred_element_type=F32)
    c = jnp.dot(ys_ref[...], ws_ref[...], preferred_element_type=F32)
    m = (jax.nn.sigmoid(g0_ref[...] + bm_ref[0:1, :]) * a
         + jax.nn.sigmoid(g1_ref[...] + bm_ref[1:2, :]) * b
         + jax.nn.sigmoid(g2_ref[...] + bm_ref[2:3, :]) * c)
    o_ref[...] = m.astype(BF16)


def _merge(y_fox, y_dsa, y_ssd, wf, wd, ws, proj, b_merge):
    t = y_fox.shape[0]
    d = wf.shape[1]
    tm = 512
    tn = 512
    gb = OFF_MG // tn
    gspec = lambda br: pl.BlockSpec((tm, tn), lambda i, j: (i, gb + br * (d // tn) + j))
    return pl.pallas_call(
        _merge_kernel,
        grid=(t // tm, d // tn),
        in_specs=[pl.BlockSpec((tm, FOX_WIDTH), lambda i, j: (i, 0)),
                  pl.BlockSpec((tm, DSA_WIDTH), lambda i, j: (i, 0)),
                  pl.BlockSpec((tm, SSD_WIDTH), lambda i, j: (i, 0)),
                  pl.BlockSpec((FOX_WIDTH, tn), lambda i, j: (0, j)),
                  pl.BlockSpec((DSA_WIDTH, tn), lambda i, j: (0, j)),
                  pl.BlockSpec((SSD_WIDTH, tn), lambda i, j: (0, j)),
                  gspec(0), gspec(1), gspec(2),
                  pl.BlockSpec((N_BRANCH, tn), lambda i, j: (0, j))],
        out_specs=pl.BlockSpec((tm, tn), lambda i, j: (i, j)),
        out_shape=jax.ShapeDtypeStruct((t, d), BF16),
        compiler_params=_cparams("parallel", "arbitrary"),
        name="branch_merge",
    )(y_fox, y_dsa, y_ssd, wf, wd, ws, proj, proj, proj, b_merge)


def _outproj_kernel(m_ref, w_ref, x_ref, g_ref, o_ref):
    out = jnp.dot(m_ref[...], w_ref[...], preferred_element_type=F32)
    o_ref[...] = x_ref[...] + g_ref[...] * out


def _outproj(merged, w_out, x2, gate, seq):
    t, d = x2.shape
    tm = 512
    tn = 512
    return pl.pallas_call(
        _outproj_kernel,
        grid=(t // tm, d // tn),
        in_specs=[pl.BlockSpec((tm, d), lambda i, j: (i, 0)),
                  pl.BlockSpec((d, tn), lambda i, j: (0, j)),
                  pl.BlockSpec((tm, tn), lambda i, j: (i, j)),
                  pl.BlockSpec((None, 1, tn), lambda i, j: (i * tm // seq, 0, j))],
        out_specs=pl.BlockSpec((tm, tn), lambda i, j: (i, j)),
        out_shape=jax.ShapeDtypeStruct((t, d), F32),
        compiler_params=_cparams("parallel", "arbitrary"),
        name="out_proj_residual",
    )(merged, w_out, x2, gate)


def _lane_block(vec, off):
    return jnp.zeros((1, LANES), F32).at[0, off:off + vec.shape[0]].set(vec.astype(F32))


def kernel(x, c, positions, norm_w, w_ada, b_ada, w_in, b_fox_f, fox_q_norm, fox_k_norm, dsa_q_norm, dsa_k_norm, conv_w, conv_b, dt_bias, a_log, d_skip, ssd_norm, b_merge, w_o_fox, w_o_dsa, w_o_ssd, w_out):
    batch, seq, d = x.shape
    depth = w_in.shape[0]
    t = batch * seq
    assert seq % 512 == 0 and d % 512 == 0

    c8 = jnp.zeros((8, d), F32).at[:batch].set(c)
    mod = _ada(c8, w_ada, b_ada)
    tabs = _rope_tables(positions.reshape(t, 1))
    tab128, tab64, tabik = tabs[0:3], tabs[3:6], tabs[6:9]

    expand = np.zeros((LANES, SSD_WIDTH), np.float32)
    for hh in range(SSD_HEADS):
        expand[MISC_DT + hh, hh * SSD_HEAD_DIM:(hh + 1) * SSD_HEAD_DIM] = 1.0
    expand = jnp.asarray(expand)

    off_dk = OFF_MG + N_BRANCH * d
    off_dv = off_dk + DSA_KV_WIDTH
    off_misc = off_dv + DSA_KV_WIDTH
    misc_blk = off_misc // LANES
    att_scale = HEAD_DIM ** -0.5

    h = x.reshape(t, d)
    for l in range(depth):
        shift = mod[l, :batch, 0:d].reshape(batch, 1, d)
        scale = mod[l, :batch, d:2 * d].reshape(batch, 1, d)
        gate = mod[l, :batch, 2 * d:3 * d].reshape(batch, 1, d)
        proj = _inproj(h, norm_w[l], scale, shift, _pack_w_in(w_in[l]), seq)

        fqn = _headprep(proj, OFF_FQ, FOX_HEADS, norm_w=fox_q_norm[l], scale=att_scale, name="fox_q")
        fkt = _headprep(proj, OFF_FK, FOX_HEADS, norm_w=fox_k_norm[l], transpose=True, name="fox_k")
        fvb = _headprep(proj, OFF_FV, FOX_HEADS, name="fox_v")
        fcol, frow = _fox_f_call(proj, _lane_block(b_fox_f[l], MISC_FF), seq, misc_blk)
        y_fox = _fox_attention(fqn, fkt, fvb, fcol, frow.reshape(FOX_HEADS, 1, t), proj, batch, seq)

        dqn = _headprep(proj, OFF_DQ, DSA_HEADS, norm_w=dsa_q_norm[l], tabs=tab128,
                        rope_half=HEAD_DIM // ROPE_FRACTION // 2, scale=att_scale, name="dsa_q")
        dkt = _headprep(proj, off_dk, DSA_KV_HEADS, norm_w=dsa_k_norm[l], tabs=tab128,
                        rope_half=HEAD_DIM // ROPE_FRACTION // 2, transpose=True, name="dsa_k")
        dvb = _headprep(proj, off_dv, DSA_KV_HEADS, name="dsa_v")
        iqb = _headprep(proj, OFF_IQ, IDX_HEADS * IDX_DIM // LANES, tabs=tab64,
                        rope_half=IDX_DIM // ROPE_FRACTION // 2, name="idx_q")
        kkt = _headprep(proj, off_misc, 1, tabs=tabik, rope_half=IDX_DIM // ROPE_FRACTION // 2,
                        dup_half=True, transpose=True, name="idx_k")
        y_dsa = _dsa_attention(dqn, dkt, dvb, iqb, kkt, proj, batch, seq, misc_blk)

        y_ssd = _ssd(proj, conv_w[l], conv_b[l], _lane_block(dt_bias[l], MISC_DT),
                     _lane_block(a_log[l], MISC_DT),
                     jnp.repeat(d_skip[l].astype(F32), SSD_HEAD_DIM).reshape(1, SSD_WIDTH),
                     ssd_norm[l], expand, batch, seq, misc_blk)

        merged = _merge(y_fox, y_dsa, y_ssd, w_o_fox[l].astype(BF16), w_o_dsa[l].astype(BF16),
                        w_o_ssd[l].astype(BF16), proj, b_merge[l])
        h = _outproj(merged, w_out[l].astype(BF16), h, gate, seq)
    return h.reshape(batch, seq, d)
```

```python
import functools
import math

import numpy as np
import jax
import jax.numpy as jnp
from jax import lax
from jax.experimental import pallas as pl
from jax.experimental.pallas import tpu as pltpu

F32 = jnp.float32
BF16 = jnp.bfloat16
HIGHEST = lax.Precision.HIGHEST

HEAD_DIM = 128
FOX_HEADS = 8
FOX_WIDTH = FOX_HEADS * HEAD_DIM
DSA_HEADS = 8
DSA_KV_HEADS = 2
DSA_WIDTH = DSA_HEADS * HEAD_DIM
DSA_KV_WIDTH = DSA_KV_HEADS * HEAD_DIM
IDX_HEADS = 16
IDX_DIM = 64
IDX_WIDTH = IDX_HEADS * IDX_DIM
TOPK_MAX = 256
SSD_HEADS = 32
SSD_HEAD_DIM = 64
SSD_WIDTH = SSD_HEADS * SSD_HEAD_DIM
SSD_GROUPS = 4
SSD_STATE = 128
SSD_CHUNK = 128
CONV_WIDTH = 4
CONV_CH = SSD_WIDTH + 2 * SSD_GROUPS * SSD_STATE
N_BRANCH = 3
ROPE_THETA = 500000.0
ROPE_FRACTION = 4
EPS = 1e-6

LANES = 128
NEG = -1e30
M_INIT = -1e29
INT_MIN = -(2 ** 31)
VMEM_LIMIT = 52 * 1024 * 1024
LOG2E = math.log2(math.e)
ATT_SCALE = HEAD_DIM ** -0.5
IDX_SCALE = (IDX_DIM ** -0.5) * (IDX_HEADS ** -0.5)
ROPE_HALF = HEAD_DIM // ROPE_FRACTION // 2
IDX_ROPE_HALF = IDX_DIM // ROPE_FRACTION // 2
FOX_AUG = 2 * HEAD_DIM

OFF_XBC = 0
OFF_FQ = OFF_XBC + CONV_CH
OFF_FK = OFF_FQ + FOX_WIDTH
OFF_FV = OFF_FK + FOX_WIDTH
OFF_FG = OFF_FV + FOX_WIDTH
OFF_DQ = OFF_FG + FOX_WIDTH
OFF_IQ = OFF_DQ + DSA_WIDTH
OFF_DG = OFF_IQ + IDX_WIDTH
OFF_SZ = OFF_DG + DSA_WIDTH
OFF_MG = OFF_SZ + SSD_WIDTH
MISC_IK = 0
MISC_FF = MISC_IK + IDX_DIM
MISC_IW = MISC_FF + FOX_HEADS
MISC_DT = MISC_IW + IDX_HEADS
IN_TN = 768
PREP_TP = 512
ATT_TK = 512
FOX_TQ = 512
FOX_TK = 2048
DSA_TQ = 256
CNT_ROWS = 64


def _cparams(*sem):
    return pltpu.CompilerParams(dimension_semantics=sem, vmem_limit_bytes=VMEM_LIMIT)


def _silu(x):
    return x * jax.nn.sigmoid(x)


def _softplus(x):
    return jnp.maximum(x, 0.0) + jnp.log1p(jnp.exp(-jnp.abs(x)))


def _log_sigmoid(x):
    return jnp.minimum(x, 0.0) - jnp.log1p(jnp.exp(-jnp.abs(x)))


def _rms(x, w):
    return x * lax.rsqrt(jnp.mean(x * x, axis=-1, keepdims=True) + EPS) * w


def _split3(x):
    x1 = x.astype(BF16)
    r = x - x1.astype(F32)
    x2 = r.astype(BF16)
    x3 = (r - x2.astype(F32)).astype(BF16)
    return x1, x2, x3


def _dot_exact_rhs(x, m_bf16):
    x1, x2, x3 = _split3(x)
    return (jnp.dot(x1, m_bf16, preferred_element_type=F32)
            + jnp.dot(x2, m_bf16, preferred_element_type=F32)
            + jnp.dot(x3, m_bf16, preferred_element_type=F32))


def _cumsum_rows(x):
    n = x.shape[0]
    r = lax.broadcasted_iota(jnp.int32, (n, n), 0)
    c = lax.broadcasted_iota(jnp.int32, (n, n), 1)
    tril = jnp.where(c <= r, 1.0, 0.0).astype(BF16)
    x1, x2, x3 = _split3(x)
    return (jnp.dot(tril, x1, preferred_element_type=F32)
            + jnp.dot(tril, x2, preferred_element_type=F32)
            + jnp.dot(tril, x3, preferred_element_type=F32))


def _rope(x, c, sa, sb, half):
    return x * c + pltpu.roll(x, LANES - half, 1) * sa + pltpu.roll(x, half, 1) * sb


def _ada_kernel(c_ref, w_ref, b_ref, o_ref):
    s = _silu(c_ref[...])
    o_ref[0] = jnp.dot(s, w_ref[0], precision=HIGHEST, preferred_element_type=F32) + b_ref[0]


def _ada(c8, w_ada, b_ada):
    depth, d, n = w_ada.shape
    tn = 512
    return pl.pallas_call(
        _ada_kernel,
        grid=(depth, n // tn),
        in_specs=[pl.BlockSpec((8, d), lambda l, j: (0, 0)),
                  pl.BlockSpec((1, d, tn), lambda l, j: (l, 0, j)),
                  pl.BlockSpec((1, 1, tn), lambda l, j: (l, 0, j))],
        out_specs=pl.BlockSpec((1, 8, tn), lambda l, j: (l, 0, j)),
        out_shape=jax.ShapeDtypeStruct((depth, 8, n), F32),
        compiler_params=_cparams("parallel", "parallel"),
        name="ada_mod",
    )(c8, w_ada, b_ada.reshape(depth, 1, n))


def _inproj_kernel(x_ref, nw_ref, sc_ref, sh_ref, w_ref, o_ref, u_ref, *, rows):
    @pl.when(pl.program_id(1) == 0)
    def _():
        nw = nw_ref[...]
        sc = 1.0 + sc_ref[...]
        sh = sh_ref[...]

        def body(r, carry):
            sl = pl.ds(pl.multiple_of(r * rows, rows), rows)
            u_ref[sl, :] = (_rms(x_ref[sl, :], nw) * sc + sh).astype(BF16)
            return carry

        lax.fori_loop(0, x_ref.shape[0] // rows, body, 0)

    o_ref[...] = jnp.dot(u_ref[...], w_ref[...], preferred_element_type=F32)


def _inproj(x2, norm_w, scale, shift, w_packed, seq):
    t, d = x2.shape
    n = w_packed.shape[1]
    tm = min(1024, seq)
    return pl.pallas_call(
        functools.partial(_inproj_kernel, rows=64),
        grid=(t // tm, n // IN_TN),
        in_specs=[pl.BlockSpec((tm, d), lambda i, j: (i, 0)),
                  pl.BlockSpec((1, d), lambda i, j: (0, 0)),
                  pl.BlockSpec((None, 1, d), lambda i, j: (i * tm // seq, 0, 0)),
                  pl.BlockSpec((None, 1, d), lambda i, j: (i * tm // seq, 0, 0)),
                  pl.BlockSpec((d, IN_TN), lambda i, j: (0, j))],
        out_specs=pl.BlockSpec((tm, IN_TN), lambda i, j: (i, j)),
        out_shape=jax.ShapeDtypeStruct((t, n), F32),
        scratch_shapes=[pltpu.VMEM((tm, d), BF16)],
        compiler_params=_cparams("parallel", "arbitrary"),
        name="norm_inproj",
    )(x2, norm_w.reshape(1, d), scale, shift, w_packed)


def _pack_w_in(w):
    sizes = (FOX_WIDTH, FOX_WIDTH, FOX_WIDTH, FOX_HEADS, FOX_WIDTH,
             DSA_WIDTH, DSA_KV_WIDTH, DSA_KV_WIDTH,
             IDX_WIDTH, IDX_DIM, IDX_HEADS, DSA_WIDTH,
             SSD_WIDTH, CONV_CH, SSD_HEADS, N_BRANCH * w.shape[0])
    offs = np.concatenate([[0], np.cumsum(sizes)])
    (fq, fk, fv, ff, fg, dq, dk, dv, iq, ik, iw, dg, sz, sxbc, sdt, mg) = [
        w[:, int(offs[i]):int(offs[i + 1])].astype(BF16) for i in range(len(sizes))]
    d = w.shape[0]
    pad8 = jnp.zeros((d, LANES - (IDX_DIM + FOX_HEADS + IDX_HEADS + SSD_HEADS)), BF16)
    cols = [sxbc, fq, fk, fv, fg, dq, iq, dg, sz, mg, dk, dv, ik, ff, iw, sdt, pad8]
    packed = jnp.concatenate(cols, axis=1)
    pad = (-packed.shape[1]) % IN_TN
    return jnp.concatenate([packed, jnp.zeros((d, pad), BF16)], axis=1)


def _rope_tab_kernel(pos_ref, c128, sa128, sb128, c64, sa64, sb64, cik, saik, sbik):
    pos = pos_ref[...].astype(F32)
    lane = lax.broadcasted_iota(jnp.int32, (8, LANES), 1)[0:1, :]

    def tables(d):
        rd = d // ROPE_FRACTION
        half = rd // 2
        j = lane % d
        inv = jnp.power(ROPE_THETA, -(j % half).astype(F32) / half)
        ang = pos * inv
        cos = jnp.cos(ang)
        sin = jnp.sin(ang)
        c = jnp.where(j < rd, cos, 1.0)
        sa = jnp.where(j < half, -sin, 0.0)
        sb = jnp.where(j < half, 0.0, jnp.where(j < rd, sin, 0.0))
        return c, sa, sb

    c, sa, sb = tables(HEAD_DIM)
    c128[...] = c
    sa128[...] = sa
    sb128[...] = sb
    c, sa, sb = tables(IDX_DIM)
    c64[...] = c
    sa64[...] = sa
    sb64[...] = sb
    keep = lane < IDX_DIM
    cik[...] = jnp.where(keep, c, 0.0)
    saik[...] = jnp.where(keep, sa, 0.0)
    sbik[...] = jnp.where(keep, sb, 0.0)


def _rope_tables(pos2):
    t = pos2.shape[0]
    tp = min(PREP_TP, t)
    spec = pl.BlockSpec((tp, LANES), lambda i: (i, 0))
    return pl.pallas_call(
        _rope_tab_kernel,
        grid=(t // tp,),
        in_specs=[pl.BlockSpec((tp, 1), lambda i: (i, 0))],
        out_specs=[spec] * 9,
        out_shape=[jax.ShapeDtypeStruct((t, LANES), F32)] * 9,
        compiler_params=_cparams("parallel"),
        name="rope_tables",
    )(pos2)


def _fox_prep_kernel(q_ref, k_ref, v_ref, misc_ref, bias_ref, qw_ref, kw_ref,
                     qt_ref, ka_ref, vt_ref, carry_ref, *, tiles_per_batch):
    @pl.when(pl.program_id(0) % tiles_per_batch == 0)
    def _():
        carry_ref[...] = jnp.zeros_like(carry_ref)

    tp = q_ref.shape[0]
    cs = _cumsum_rows(_log_sigmoid(misc_ref[...] + bias_ref[...])) + carry_ref[0:1, :]
    carry_ref[...] = jnp.broadcast_to(cs[tp - 1:tp, :], carry_ref.shape)
    f = cs * LOG2E
    f1 = f.astype(BF16).astype(F32)
    r1 = f - f1
    f2 = r1.astype(BF16).astype(F32)
    f3 = r1 - f2
    cols = (f1, f2, f3)
    rows = tuple(x.T for x in cols)
    sub = lax.broadcasted_iota(jnp.int32, (8, tp), 0)
    lane = lax.broadcasted_iota(jnp.int32, (tp, LANES), 1)
    zeros_t = jnp.zeros((HEAD_DIM - 8, tp), F32)
    qw = qw_ref[...] * (ATT_SCALE * LOG2E)
    kw = kw_ref[...]
    for h in range(FOX_HEADS):
        hs = slice(h * HEAD_DIM, (h + 1) * HEAD_DIM)
        hl = MISC_FF + h
        qn = _rms(q_ref[:, hs], qw)
        aug = jnp.where(sub == 0, rows[0][hl:hl + 1, :],
                        jnp.where(sub == 1, rows[1][hl:hl + 1, :],
                                  jnp.where(sub == 2, rows[2][hl:hl + 1, :],
                                            jnp.where(sub < 6, 1.0, 0.0))))
        qt_ref[h * FOX_AUG:h * FOX_AUG + HEAD_DIM, :] = qn.T.astype(BF16)
        qt_ref[h * FOX_AUG + HEAD_DIM:(h + 1) * FOX_AUG, :] = jnp.concatenate(
            [aug, zeros_t], axis=0).astype(BF16)
        ka_ref[:, h * FOX_AUG:h * FOX_AUG + HEAD_DIM] = _rms(k_ref[:, hs], kw).astype(BF16)
        kaug = jnp.where(lane < 3, 1.0,
                         jnp.where(lane == 3, -cols[0][:, hl:hl + 1],
                                   jnp.where(lane == 4, -cols[1][:, hl:hl + 1],
                                             jnp.where(lane == 5, -cols[2][:, hl:hl + 1], 0.0))))
        ka_ref[:, h * FOX_AUG + HEAD_DIM:(h + 1) * FOX_AUG] = kaug.astype(BF16)
        vt_ref[hs, :] = v_ref[:, hs].T.astype(BF16)


def _fox_prep(proj, bias128, qw, kw, seq, misc_blk):
    t = proj.shape[0]
    tp = min(PREP_TP, seq)
    wide = lambda off: pl.BlockSpec((tp, FOX_WIDTH), lambda i: (i, off // FOX_WIDTH))
    row = pl.BlockSpec((1, LANES), lambda i: (0, 0))
    return pl.pallas_call(
        functools.partial(_fox_prep_kernel, tiles_per_batch=seq // tp),
        grid=(t // tp,),
        in_specs=[wide(OFF_FQ), wide(OFF_FK), wide(OFF_FV),
                  pl.BlockSpec((tp, LANES), lambda i: (i, misc_blk)), row, row, row],
        out_specs=[pl.BlockSpec((FOX_HEADS * FOX_AUG, tp), lambda i: (0, i)),
                   pl.BlockSpec((tp, FOX_HEADS * FOX_AUG), lambda i: (i, 0)),
                   pl.BlockSpec((FOX_WIDTH, tp), lambda i: (0, i))],
        out_shape=[jax.ShapeDtypeStruct((FOX_HEADS * FOX_AUG, t), BF16),
                   jax.ShapeDtypeStruct((t, FOX_HEADS * FOX_AUG), BF16),
                   jax.ShapeDtypeStruct((FOX_WIDTH, t), BF16)],
        scratch_shapes=[pltpu.VMEM((8, LANES), F32)],
        compiler_params=_cparams("arbitrary"),
        name="fox_prep",
    )(proj, proj, proj, proj, bias128, qw.reshape(1, LANES), kw.reshape(1, LANES))


def _fox_kernel(qt_ref, ka_ref, vt_ref, g_ref, o_ref, acc_ref, *, tq, tk):
    i = pl.program_id(2)
    qt = qt_ref[...]
    acc_ref[...] = jnp.zeros_like(acc_ref)

    def update(s, start, size, carry):
        m, l = carry
        m_new = jnp.maximum(m, jnp.max(s, axis=0, keepdims=True))
        alpha = jnp.exp2(m - m_new)
        p = jnp.exp2(s - m_new)
        l = alpha * l + jnp.sum(p, axis=0, keepdims=True)
        acc_ref[...] = alpha * acc_ref[...] + jnp.dot(
            vt_ref[:, pl.ds(start, size)], p.astype(BF16), preferred_element_type=F32)
        return m_new, l

    def full_tile(j, carry):
        start = pl.multiple_of(j * tk, tk)
        s = jnp.dot(ka_ref[pl.ds(start, tk), :], qt, preferred_element_type=F32)
        return update(s, start, tk, carry)

    per = tk // tq
    nfull = i // per
    rem = i - nfull * per
    init = (jnp.full((1, tq), NEG, F32), jnp.zeros((1, tq), F32))
    carry = lax.fori_loop(0, nfull, full_tile, init)
    start = pl.multiple_of(nfull * tk, tk)
    for r in range(per):
        @pl.when(rem == r)
        def _():
            size = (r + 1) * tq
            s = jnp.dot(ka_ref[pl.ds(start, size), :], qt, preferred_element_type=F32)
            krow = lax.broadcasted_iota(jnp.int32, (size, tq), 0)
            qcol = lax.broadcasted_iota(jnp.int32, (size, tq), 1) + r * tq
            _, l = update(jnp.where(krow <= qcol, s, NEG), start, size, carry)
            o_ref[...] = ((acc_ref[...] / l).T * _silu(g_ref[...])).astype(BF16)


def _fox_attention(qt, ka, vt, proj, batch, seq):
    t = ka.shape[0]
    tq = min(FOX_TQ, seq)
    tk = min(FOX_TK, seq)
    nq = seq // tq
    gblk = OFF_FG // LANES
    return pl.pallas_call(
        functools.partial(_fox_kernel, tq=tq, tk=tk),
        grid=(batch, FOX_HEADS, nq),
        in_specs=[pl.BlockSpec((FOX_AUG, tq), lambda b, h, i: (h, b * nq + i)),
                  pl.BlockSpec((seq, FOX_AUG), lambda b, h, i: (b, h)),
                  pl.BlockSpec((HEAD_DIM, seq), lambda b, h, i: (h, b)),
                  pl.BlockSpec((tq, LANES), lambda b, h, i: (b * nq + i, gblk + h))],
        out_specs=pl.BlockSpec((tq, LANES), lambda b, h, i: (b * nq + i, h)),
        out_shape=jax.ShapeDtypeStruct((t, FOX_WIDTH), BF16),
        scratch_shapes=[pltpu.VMEM((HEAD_DIM, tq), F32)],
        compiler_params=_cparams("parallel", "parallel", "arbitrary"),
        name="fox_attention",
    )(qt, ka, vt, proj)


def _dsa_prep_kernel(dq_ref, iq_ref, kv_ref, misc_ref, c128, sa128, sb128, c64, sa64, sb64,
                     cik, saik, sbik, qw_ref, kw_ref, dqt_ref, iqt_ref, dk_ref, dvt_ref, ki_ref):
    t128 = (c128[...], sa128[...], sb128[...])
    t64 = (c64[...], sa64[...], sb64[...])
    qw = qw_ref[...] * (ATT_SCALE * LOG2E)
    kw = kw_ref[...]
    for h in range(DSA_HEADS):
        hs = slice(h * HEAD_DIM, (h + 1) * HEAD_DIM)
        dqt_ref[hs, :] = _rope(_rms(dq_ref[:, hs], qw), *t128, ROPE_HALF).T.astype(BF16)
    for p in range(IDX_WIDTH // LANES):
        ps = slice(p * LANES, (p + 1) * LANES)
        iqt_ref[ps, :] = _rope(iq_ref[:, ps], *t64, IDX_ROPE_HALF).T.astype(BF16)
    for g in range(DSA_KV_HEADS):
        gs = slice(g * HEAD_DIM, (g + 1) * HEAD_DIM)
        dk_ref[:, gs] = _rope(_rms(kv_ref[:, gs], kw), *t128, ROPE_HALF).astype(BF16)
        dvt_ref[gs, :] = kv_ref[:, DSA_KV_WIDTH + g * HEAD_DIM:
                                DSA_KV_WIDTH + (g + 1) * HEAD_DIM].T.astype(BF16)
    ik = _rope(misc_ref[...], cik[...], saik[...], sbik[...], IDX_ROPE_HALF)
    ki_ref[...] = (ik + pltpu.roll(ik, LANES // 2, 1)).astype(BF16)


def _dsa_prep(proj, tabs, qw, kw, seq, off_dk, misc_blk):
    t = proj.shape[0]
    tp = min(PREP_TP, seq)
    tab = pl.BlockSpec((tp, LANES), lambda i: (i, 0))
    row = pl.BlockSpec((1, LANES), lambda i: (0, 0))
    return pl.pallas_call(
        _dsa_prep_kernel,
        grid=(t // tp,),
        in_specs=[pl.BlockSpec((tp, DSA_WIDTH), lambda i: (i, OFF_DQ // DSA_WIDTH)),
                  pl.BlockSpec((tp, IDX_WIDTH), lambda i: (i, OFF_IQ // IDX_WIDTH)),
                  pl.BlockSpec((tp, 2 * DSA_KV_WIDTH), lambda i: (i, off_dk // (2 * DSA_KV_WIDTH))),
                  pl.BlockSpec((tp, LANES), lambda i: (i, misc_blk))] + [tab] * 9 + [row, row],
        out_specs=[pl.BlockSpec((DSA_WIDTH, tp), lambda i: (0, i)),
                   pl.BlockSpec((IDX_WIDTH, tp), lambda i: (0, i)),
                   pl.BlockSpec((tp, DSA_KV_WIDTH), lambda i: (i, 0)),
                   pl.BlockSpec((DSA_KV_WIDTH, tp), lambda i: (0, i)),
                   pl.BlockSpec((tp, LANES), lambda i: (i, 0))],
        out_shape=[jax.ShapeDtypeStruct((DSA_WIDTH, t), BF16),
                   jax.ShapeDtypeStruct((IDX_WIDTH, t), BF16),
                   jax.ShapeDtypeStruct((t, DSA_KV_WIDTH), BF16),
                   jax.ShapeDtypeStruct((DSA_KV_WIDTH, t), BF16),
                   jax.ShapeDtypeStruct((t, LANES), BF16)],
        compiler_params=_cparams("parallel"),
        name="dsa_prep",
    )(proj, proj, proj, proj, *tabs, qw.reshape(1, LANES), kw.reshape(1, LANES))


def _dsa_kernel(dqt_ref, dk_ref, dvt_ref, iqt_ref, ki_ref, misc_ref, g_ref, o_ref,
                keys_ref, qm_ref, acc_ref, *, tq, tk, topk):
    i = pl.program_id(1)
    q0 = i * tq
    nkt = (q0 + tq - 1) // tk + 1
    krow = lax.broadcasted_iota(jnp.int32, (tk, tq), 0)
    qcol = q0 + lax.broadcasted_iota(jnp.int32, (tk, tq), 1)
    w_t = (misc_ref[...] * IDX_SCALE).T

    zero = jnp.zeros((LANES - IDX_DIM, tq), BF16)
    for hh in range(IDX_HEADS):
        qm_ref[hh] = jnp.concatenate([iqt_ref[hh * IDX_DIM:(hh + 1) * IDX_DIM, :], zero], axis=0)

    def score_tile(j, carry):
        start = pl.multiple_of(j * tk, tk)
        kk = ki_ref[pl.ds(start, tk), :]
        acc = jnp.zeros((tk, tq), F32)
        for hh in range(IDX_HEADS):
            s = jnp.dot(kk, qm_ref[hh], preferred_element_type=F32)
            acc = acc + w_t[MISC_IW + hh:MISC_IW + hh + 1, :] * jnp.maximum(s, 0.0)
        bits = pltpu.bitcast(acc, jnp.int32)
        key = bits ^ ((bits >> 31) & 0x7FFFFFFF)
        keys_ref[pl.ds(start, tk), :] = jnp.where(start + krow <= qcol, key, INT_MIN)
        return carry

    lax.fori_loop(0, nkt, score_tile, 0)

    def count_ge(cand):
        def body(j, c):
            start = pl.multiple_of(j * tk, tk)
            ge = jnp.where(keys_ref[pl.ds(start, tk), :] >= cand, 1.0, 0.0)
            return c + jnp.sum(ge.reshape(tk // CNT_ROWS, CNT_ROWS, tq), axis=0)
        c = lax.fori_loop(0, nkt, body, jnp.zeros((CNT_ROWS, tq), F32))
        return jnp.sum(c, axis=0, keepdims=True)

    def search(p, thr):
        cand = thr + jnp.left_shift(jnp.int32(1), 31 - p)
        return jnp.where(count_ge(cand) >= topk, cand, thr)

    thr = lax.fori_loop(0, 32, search, jnp.full((1, tq), INT_MIN, jnp.int32))

    cnt_ge = count_ge(thr)
    cnt_gt = count_ge(thr + 1)
    tied = jnp.where((cnt_ge > topk) & (thr > INT_MIN), 1.0, 0.0)

    @pl.when(jnp.max(tied) > 0.0)
    def _():
        quota = topk - cnt_gt
        ur = lax.broadcasted_iota(jnp.int32, (tk, tk), 0)
        uc = lax.broadcasted_iota(jnp.int32, (tk, tk), 1)
        tril = jnp.where(uc <= ur, 1.0, 0.0).astype(BF16)

        def fix(j, seen):
            start = pl.multiple_of(j * tk, tk)
            k = keys_ref[pl.ds(start, tk), :]
            eq = jnp.where(k == thr, 1.0, 0.0)
            rank = seen + jnp.dot(tril, eq.astype(BF16), preferred_element_type=F32)
            drop = (eq * tied) * jnp.where(rank > quota, 1.0, 0.0)
            keys_ref[pl.ds(start, tk), :] = jnp.where(drop > 0.0, thr - 1, k)
            return seen + jnp.sum(eq, axis=0, keepdims=True)

        lax.fori_loop(0, nkt, fix, jnp.zeros((1, tq), F32))

    thr = jnp.maximum(thr, INT_MIN + 1)

    rep = DSA_HEADS // DSA_KV_HEADS
    qs = [jnp.concatenate([dqt_ref[(g * rep + r) * HEAD_DIM:(g * rep + r + 1) * HEAD_DIM, :]
                           for r in range(rep)], axis=1) for g in range(DSA_KV_HEADS)]
    acc_ref[...] = jnp.zeros_like(acc_ref)

    def attend(j, carry):
        start = pl.multiple_of(j * tk, tk)
        nb = jnp.where(keys_ref[pl.ds(start, tk), :] >= thr, 0.0, NEG)
        nb = jnp.concatenate([nb] * rep, axis=1)
        out = []
        for g in range(DSA_KV_HEADS):
            m, l = carry[2 * g:2 * g + 2]
            gs = slice(g * HEAD_DIM, (g + 1) * HEAD_DIM)
            s = jnp.dot(dk_ref[pl.ds(start, tk), gs], qs[g], preferred_element_type=F32) + nb
            m_new = jnp.maximum(m, jnp.max(s, axis=0, keepdims=True))
            alpha = jnp.exp2(m - m_new)
            p = jnp.exp2(s - m_new)
            l = alpha * l + jnp.sum(p, axis=0, keepdims=True)
            acc_ref[g] = alpha * acc_ref[g] + jnp.dot(
                dvt_ref[gs, pl.ds(start, tk)], p.astype(BF16), preferred_element_type=F32)
            out.extend([m_new, l])
        return tuple(out)

    init = (jnp.full((1, rep * tq), M_INIT, F32), jnp.zeros((1, rep * tq), F32)) * DSA_KV_HEADS
    res = lax.fori_loop(0, nkt, attend, init)
    for g in range(DSA_KV_HEADS):
        o = acc_ref[g] / res[2 * g + 1]
        for r in range(rep):
            sl = slice((g * rep + r) * HEAD_DIM, (g * rep + r + 1) * HEAD_DIM)
            o_ref[:, sl] = (o[:, r * tq:(r + 1) * tq].T * _silu(g_ref[:, sl])).astype(BF16)


def _dsa_attention(dqt, dk, dvt, iqt, ki, proj, batch, seq, misc_blk):
    t = dk.shape[0]
    tq = min(DSA_TQ, seq)
    tk = min(ATT_TK, seq)
    nq = seq // tq
    topk = min(TOPK_MAX, seq // 4)
    rep = DSA_HEADS // DSA_KV_HEADS
    return pl.pallas_call(
        functools.partial(_dsa_kernel, tq=tq, tk=tk, topk=topk),
        grid=(batch, nq),
        in_specs=[pl.BlockSpec((DSA_WIDTH, tq), lambda b, i: (0, b * nq + i)),
                  pl.BlockSpec((seq, DSA_KV_WIDTH), lambda b, i: (b, 0)),
                  pl.BlockSpec((DSA_KV_WIDTH, seq), lambda b, i: (0, b)),
                  pl.BlockSpec((IDX_WIDTH, tq), lambda b, i: (0, b * nq + i)),
                  pl.BlockSpec((seq, LANES), lambda b, i: (b, 0)),
                  pl.BlockSpec((tq, LANES), lambda b, i: (b * nq + i, misc_blk)),
                  pl.BlockSpec((tq, DSA_WIDTH), lambda b, i: (b * nq + i, OFF_DG // DSA_WIDTH))],
        out_specs=pl.BlockSpec((tq, DSA_WIDTH), lambda b, i: (b * nq + i, 0)),
        out_shape=jax.ShapeDtypeStruct((t, DSA_WIDTH), BF16),
        scratch_shapes=[pltpu.VMEM((seq, tq), jnp.int32),
                        pltpu.VMEM((IDX_HEADS, LANES, tq), BF16),
                        pltpu.VMEM((DSA_KV_HEADS, HEAD_DIM, rep * tq), F32)],
        compiler_params=_cparams("parallel", "arbitrary"),
        name="dsa_attention",
    )(dqt, dk, dvt, iqt, ki, proj, proj)


def _ssd_kernel(xin_ref, z_ref, misc_ref, cw_ref, cb_ref, dtb_ref, alog_ref, dsk_ref, nw_ref,
                e_ref, o_ref, xbuf, xbc, ybuf, state, *, L):
    @pl.when(pl.program_id(1) == 0)
    def _():
        xbuf[0:8, :] = jnp.zeros((8, CONV_CH), F32)
        state[...] = jnp.zeros_like(state)

    xbuf[8:8 + L, :] = xin_ref[...]
    lc = 512
    for ch in range(CONV_CH // lc):
        ls = slice(ch * lc, (ch + 1) * lc)
        acc = cb_ref[:, ls] + cw_ref[3:4, ls] * xbuf[8:8 + L, ls]
        acc = acc + cw_ref[2:3, ls] * xbuf[7:7 + L, ls]
        acc = acc + cw_ref[1:2, ls] * xbuf[6:6 + L, ls]
        acc = acc + cw_ref[0:1, ls] * xbuf[5:5 + L, ls]
        xbc[:, ls] = _silu(acc)
    xbuf[0:8, :] = xbuf[L:L + 8, :]

    dt = _softplus(misc_ref[...] + dtb_ref[...])
    a = -jnp.exp(alog_ref[...])
    acs = _cumsum_rows(dt * a)
    ea = jnp.exp(acs)
    dte = jnp.exp(acs[L - 1:L, :] - acs)
    full = _dot_exact_rhs(jnp.concatenate([dt, ea, dte], axis=0), e_ref[...])
    dt_f = full[0:L]
    ea_f = full[L:2 * L]
    dte_f = full[2 * L:3 * L]
    acs_t = acs.T
    lane = lax.broadcasted_iota(jnp.int32, (L, LANES), 1)
    r = lax.broadcasted_iota(jnp.int32, (L, L), 0)
    c = lax.broadcasted_iota(jnp.int32, (L, L), 1)
    causal = c <= r

    gw = SSD_WIDTH // SSD_GROUPS
    hpg = SSD_HEADS // SSD_GROUPS
    for g in range(SSD_GROUPS):
        gs = slice(g * gw, (g + 1) * gw)
        xd = xbc[:, gs] * dt_f[:, gs]
        xd_b = xd.astype(BF16)
        bg = xbc[:, SSD_WIDTH + g * SSD_STATE:SSD_WIDTH + (g + 1) * SSD_STATE]
        cg = xbc[:, SSD_WIDTH + (SSD_GROUPS + g) * SSD_STATE:
                 SSD_WIDTH + (SSD_GROUPS + g + 1) * SSD_STATE].astype(BF16)
        cbm = lax.dot_general(cg, bg.astype(BF16), (((1,), (1,)), ((), ())),
                              preferred_element_type=F32)
        for pr in range(hpg // 2):
            xpair = xd_b[:, pr * LANES:(pr + 1) * LANES]
            ys = []
            for hh in range(2):
                hl = MISC_DT + g * hpg + 2 * pr + hh
                seg = jnp.where(causal, acs[:, hl:hl + 1] - acs_t[hl:hl + 1, :], NEG)
                mm = (cbm * jnp.exp(seg)).astype(BF16)
                ys.append(jnp.dot(mm, xpair, preferred_element_type=F32))
            ybuf[:, g * gw + pr * LANES:g * gw + (pr + 1) * LANES] = jnp.where(
                lane < SSD_HEAD_DIM, ys[0], ys[1])
        st = state[:, gs]
        yoff = jnp.dot(cg, st.astype(BF16), preferred_element_type=F32)
        ybuf[:, gs] = ybuf[:, gs] + yoff * ea_f[:, gs]
        zz = (xd * dte_f[:, gs]).astype(BF16)
        state[:, gs] = st * ea_f[L - 1:L, gs] + jnp.dot(bg.T.astype(BF16), zz,
                                                        preferred_element_type=F32)
        y = ybuf[:, gs] + dsk_ref[:, gs] * xbc[:, gs]
        y = y * _silu(z_ref[:, gs])
        o_ref[:, gs] = _rms(y, nw_ref[:, gs]).astype(BF16)


def _ssd(proj, conv_w, conv_b, dtb128, alog128, dskip_f, ssd_norm, expand, batch, seq, misc_blk):
    t = proj.shape[0]
    L = SSD_CHUNK
    nc = seq // L
    const = lambda shape: pl.BlockSpec(shape, lambda b, c: (0, 0))
    return pl.pallas_call(
        functools.partial(_ssd_kernel, L=L),
        grid=(batch, nc),
        in_specs=[pl.BlockSpec((L, CONV_CH), lambda b, c: (b * nc + c, 0)),
                  pl.BlockSpec((L, SSD_WIDTH), lambda b, c: (b * nc + c, OFF_SZ // SSD_WIDTH)),
                  pl.BlockSpec((L, LANES), lambda b, c: (b * nc + c, misc_blk)),
                  const((CONV_WIDTH, CONV_CH)), const((1, CONV_CH)), const((1, LANES)),
                  const((1, LANES)), const((1, SSD_WIDTH)), const((1, SSD_WIDTH)),
                  const((LANES, SSD_WIDTH))],
        out_specs=pl.BlockSpec((L, SSD_WIDTH), lambda b, c: (b * nc + c, 0)),
        out_shape=jax.ShapeDtypeStruct((t, SSD_WIDTH), BF16),
        scratch_shapes=[pltpu.VMEM((L + 8, CONV_CH), F32), pltpu.VMEM((L, CONV_CH), F32),
                        pltpu.VMEM((L, SSD_WIDTH), F32), pltpu.VMEM((SSD_STATE, SSD_WIDTH), F32)],
        compiler_params=_cparams("parallel", "arbitrary"),
        name="ssd_scan",
    )(proj, proj, proj, conv_w, conv_b.reshape(1, CONV_CH), dtb128, alog128, dskip_f,
      ssd_norm.reshape(1, SSD_WIDTH), expand)


def _merge_kernel(yf_ref, yd_ref, ys_ref, wf_ref, wd_ref, ws_ref, g0_ref, g1_ref, g2_ref,
                  bm_ref, o_ref):
    a = jnp.dot(yf_ref[...], wf_ref[...], preferred_element_type=F32)
    b = jnp.dot(yd_ref[...], wd_ref[...], preferred_element_type=F32)
    c = jnp.dot(ys_ref[...], ws_ref[...], preferred_element_type=F32)
    m = (jax.nn.sigmoid(g0_ref[...] + bm_ref[0:1, :]) * a
         + jax.nn.sigmoid(g1_ref[...] + bm_ref[1:2, :]) * b
         + jax.nn.sigmoid(g2_ref[...] + bm_ref[2:3, :]) * c)
    o_ref[...] = m.astype(BF16)


def _merge(y_fox, y_dsa, y_ssd, wf, wd, ws, proj, b_merge):
    t = y_fox.shape[0]
    d = wf.shape[1]
    tm = 512
    tn = 512
    gb = OFF_MG // tn
    gspec = lambda br: pl.BlockSpec((tm, tn), lambda i, j: (i, gb + br * (d // tn) + j))
    return pl.pallas_call(
        _merge_kernel,
        grid=(t // tm, d // tn),
        in_specs=[pl.BlockSpec((tm, FOX_WIDTH), lambda i, j: (i, 0)),
                  pl.BlockSpec((tm, DSA_WIDTH), lambda i, j: (i, 0)),
                  pl.BlockSpec((tm, SSD_WIDTH), lambda i, j: (i, 0)),
                  pl.BlockSpec((FOX_WIDTH, tn), lambda i, j: (0, j)),
                  pl.BlockSpec((DSA_WIDTH, tn), lambda i, j: (0, j)),
                  pl.BlockSpec((SSD_WIDTH, tn), lambda i, j: (0, j)),
                  gspec(0), gspec(1), gspec(2),
                  pl.BlockSpec((N_BRANCH, tn), lambda i, j: (0, j))],
        out_specs=pl.BlockSpec((tm, tn), lambda i, j: (i, j)),
        out_shape=jax.ShapeDtypeStruct((t, d), BF16),
        compiler_params=_cparams("parallel", "arbitrary"),
        name="branch_merge",
    )(y_fox, y_dsa, y_ssd, wf, wd, ws, proj, proj, proj, b_merge)


def _outproj_kernel(m_ref, w_ref, x_ref, g_ref, o_ref):
    out = jnp.dot(m_ref[...], w_ref[...], preferred_element_type=F32)
    o_ref[...] = x_ref[...] + g_ref[...] * out


def _outproj(merged, w_out, x2, gate, seq):
    t, d = x2.shape
    tm = 512
    tn = 512
    return pl.pallas_call(
        _outproj_kernel,
        grid=(t // tm, d // tn),
        in_specs=[pl.BlockSpec((tm, d), lambda i, j: (i, 0)),
                  pl.BlockSpec((d, tn), lambda i, j: (0, j)),
                  pl.BlockSpec((tm, tn), lambda i, j: (i, j)),
                  pl.BlockSpec((None, 1, tn), lambda i, j: (i * tm // seq, 0, j))],
        out_specs=pl.BlockSpec((tm, tn), lambda i, j: (i, j)),
        out_shape=jax.ShapeDtypeStruct((t, d), F32),
        compiler_params=_cparams("parallel", "arbitrary"),
        name="out_proj_residual",
    )(merged, w_out, x2, gate)


def _lane_block(vec, off):
    return jnp.zeros((1, LANES), F32).at[0, off:off + vec.shape[0]].set(vec.astype(F32))


def kernel(x, c, positions, norm_w, w_ada, b_ada, w_in, b_fox_f, fox_q_norm, fox_k_norm, dsa_q_norm, dsa_k_norm, conv_w, conv_b, dt_bias, a_log, d_skip, ssd_norm, b_merge, w_o_fox, w_o_dsa, w_o_ssd, w_out):
    batch, seq, d = x.shape
    depth = w_in.shape[0]
    t = batch * seq
    assert seq % 512 == 0 and d % 512 == 0

    c8 = jnp.zeros((8, d), F32).at[:batch].set(c)
    mod = _ada(c8, w_ada, b_ada)
    tabs = _rope_tables(positions.reshape(t, 1))

    expand = np.zeros((LANES, SSD_WIDTH), np.float32)
    for hh in range(SSD_HEADS):
        expand[MISC_DT + hh, hh * SSD_HEAD_DIM:(hh + 1) * SSD_HEAD_DIM] = 1.0
    expand = jnp.asarray(expand, BF16)

    off_dk = OFF_MG + N_BRANCH * d
    misc_blk = (off_dk + 2 * DSA_KV_WIDTH) // LANES

    h = x.reshape(t, d)
    for l in range(depth):
        shift = mod[l, :batch, 0:d].reshape(batch, 1, d)
        scale = mod[l, :batch, d:2 * d].reshape(batch, 1, d)
        gate = mod[l, :batch, 2 * d:3 * d].reshape(batch, 1, d)
        proj = _inproj(h, norm_w[l], scale, shift, _pack_w_in(w_in[l]), seq)

        fqt, fka, fvt = _fox_prep(proj, _lane_block(b_fox_f[l], MISC_FF), fox_q_norm[l],
                                  fox_k_norm[l], seq, misc_blk)
        y_fox = _fox_attention(fqt, fka, fvt, proj, batch, seq)

        dqt, iqt, dk, dvt, ki = _dsa_prep(proj, tabs, dsa_q_norm[l], dsa_k_norm[l], seq,
                                          off_dk, misc_blk)
        y_dsa = _dsa_attention(dqt, dk, dvt, iqt, ki, proj, batch, seq, misc_blk)

        y_ssd = _ssd(proj, conv_w[l], conv_b[l], _lane_block(dt_bias[l], MISC_DT),
                     _lane_block(a_log[l], MISC_DT),
                     jnp.repeat(d_skip[l].astype(F32), SSD_HEAD_DIM).reshape(1, SSD_WIDTH),
                     ssd_norm[l], expand, batch, seq, misc_blk)

        merged = _merge(y_fox, y_dsa, y_ssd, w_o_fox[l].astype(BF16), w_o_dsa[l].astype(BF16),
                        w_o_ssd[l].astype(BF16), proj, b_merge[l])
        h = _outproj(merged, w_out[l].astype(BF16), h, gate, seq)
    return h.reshape(batch, seq, d)
```

```python
import functools
import math

import numpy as np
import jax
import jax.numpy as jnp
from jax import lax
from jax.experimental import pallas as pl
from jax.experimental.pallas import tpu as pltpu

F32 = jnp.float32
BF16 = jnp.bfloat16
HIGHEST = lax.Precision.HIGHEST

HEAD_DIM = 128
FOX_HEADS = 8
FOX_WIDTH = FOX_HEADS * HEAD_DIM
DSA_HEADS = 8
DSA_KV_HEADS = 2
DSA_WIDTH = DSA_HEADS * HEAD_DIM
DSA_KV_WIDTH = DSA_KV_HEADS * HEAD_DIM
IDX_HEADS = 16
IDX_DIM = 64
IDX_WIDTH = IDX_HEADS * IDX_DIM
TOPK_MAX = 256
SSD_HEADS = 32
SSD_HEAD_DIM = 64
SSD_WIDTH = SSD_HEADS * SSD_HEAD_DIM
SSD_GROUPS = 4
SSD_STATE = 128
SSD_CHUNK = 128
CONV_WIDTH = 4
CONV_CH = SSD_WIDTH + 2 * SSD_GROUPS * SSD_STATE
N_BRANCH = 3
ROPE_THETA = 500000.0
ROPE_FRACTION = 4
EPS = 1e-6

LANES = 128
NEG = -1e30
M_INIT = -1e29
INT_MIN = -(2 ** 31)
VMEM_LIMIT = 52 * 1024 * 1024
LOG2E = math.log2(math.e)
ATT_SCALE = HEAD_DIM ** -0.5
IDX_SCALE = (IDX_DIM ** -0.5) * (IDX_HEADS ** -0.5)
ROPE_HALF = HEAD_DIM // ROPE_FRACTION // 2
IDX_ROPE_HALF = IDX_DIM // ROPE_FRACTION // 2
FOX_AUG = 2 * HEAD_DIM

OFF_XBC = 0
OFF_FQ = OFF_XBC + CONV_CH
OFF_FK = OFF_FQ + FOX_WIDTH
OFF_FV = OFF_FK + FOX_WIDTH
OFF_FG = OFF_FV + FOX_WIDTH
OFF_DQ = OFF_FG + FOX_WIDTH
OFF_IQ = OFF_DQ + DSA_WIDTH
OFF_DG = OFF_IQ + IDX_WIDTH
OFF_SZ = OFF_DG + DSA_WIDTH
OFF_MG = OFF_SZ + SSD_WIDTH
MISC_IK = 0
MISC_FF = MISC_IK + IDX_DIM
MISC_IW = MISC_FF + FOX_HEADS
MISC_DT = MISC_IW + IDX_HEADS
IN_TN = 1280
OUT_TM = 512
OUT_TN = 1024
PREP_TP = 512
ATT_TK = 512
FOX_TQ = 512
FOX_TK = 2048
DSA_TQ = 256
CNT_ROWS = 64


def _cparams(*sem):
    return pltpu.CompilerParams(dimension_semantics=sem, vmem_limit_bytes=VMEM_LIMIT)


def _silu(x):
    return x * jax.nn.sigmoid(x)


def _softplus(x):
    return jnp.maximum(x, 0.0) + jnp.log1p(jnp.exp(-jnp.abs(x)))


def _log_sigmoid(x):
    return jnp.minimum(x, 0.0) - jnp.log1p(jnp.exp(-jnp.abs(x)))


def _rms(x, w):
    return x * lax.rsqrt(jnp.mean(x * x, axis=-1, keepdims=True) + EPS) * w


def _split3(x):
    x1 = x.astype(BF16)
    r = x - x1.astype(F32)
    x2 = r.astype(BF16)
    x3 = (r - x2.astype(F32)).astype(BF16)
    return x1, x2, x3


def _dot_exact_rhs(x, m_bf16):
    x1, x2, x3 = _split3(x)
    return (jnp.dot(x1, m_bf16, preferred_element_type=F32)
            + jnp.dot(x2, m_bf16, preferred_element_type=F32)
            + jnp.dot(x3, m_bf16, preferred_element_type=F32))


def _cumsum_rows(x):
    n = x.shape[0]
    r = lax.broadcasted_iota(jnp.int32, (n, n), 0)
    c = lax.broadcasted_iota(jnp.int32, (n, n), 1)
    tril = jnp.where(c <= r, 1.0, 0.0).astype(BF16)
    x1, x2, x3 = _split3(x)
    return (jnp.dot(tril, x1, preferred_element_type=F32)
            + jnp.dot(tril, x2, preferred_element_type=F32)
            + jnp.dot(tril, x3, preferred_element_type=F32))


def _rope(x, c, sa, sb, half):
    return x * c + pltpu.roll(x, LANES - half, 1) * sa + pltpu.roll(x, half, 1) * sb


def _ada_kernel(c_ref, w_ref, b_ref, o_ref):
    s = _silu(c_ref[...])
    o_ref[0] = jnp.dot(s, w_ref[0], precision=HIGHEST, preferred_element_type=F32) + b_ref[0]


def _ada(c8, w_ada, b_ada):
    depth, d, n = w_ada.shape
    tn = 512
    return pl.pallas_call(
        _ada_kernel,
        grid=(depth, n // tn),
        in_specs=[pl.BlockSpec((8, d), lambda l, j: (0, 0)),
                  pl.BlockSpec((1, d, tn), lambda l, j: (l, 0, j)),
                  pl.BlockSpec((1, 1, tn), lambda l, j: (l, 0, j))],
        out_specs=pl.BlockSpec((1, 8, tn), lambda l, j: (l, 0, j)),
        out_shape=jax.ShapeDtypeStruct((depth, 8, n), F32),
        compiler_params=_cparams("parallel", "parallel"),
        name="ada_mod",
    )(c8, w_ada, b_ada.reshape(depth, 1, n))


def _inproj_kernel(x_ref, nw_ref, sc_ref, sh_ref, w_ref, o_ref, u_ref, *, rows):
    @pl.when(pl.program_id(1) == 0)
    def _():
        nw = nw_ref[...]
        sc = 1.0 + sc_ref[...]
        sh = sh_ref[...]

        def body(r, carry):
            sl = pl.ds(pl.multiple_of(r * rows, rows), rows)
            u_ref[sl, :] = (_rms(x_ref[sl, :], nw) * sc + sh).astype(BF16)
            return carry

        lax.fori_loop(0, x_ref.shape[0] // rows, body, 0)

    o_ref[...] = jnp.dot(u_ref[...], w_ref[...], preferred_element_type=F32)


def _inproj(x2, norm_w, scale, shift, w_packed, seq):
    t, d = x2.shape
    n = w_packed.shape[1]
    tm = min(1024, seq)
    return pl.pallas_call(
        functools.partial(_inproj_kernel, rows=64),
        grid=(t // tm, n // IN_TN),
        in_specs=[pl.BlockSpec((tm, d), lambda i, j: (i, 0)),
                  pl.BlockSpec((1, d), lambda i, j: (0, 0)),
                  pl.BlockSpec((None, 1, d), lambda i, j: (i * tm // seq, 0, 0)),
                  pl.BlockSpec((None, 1, d), lambda i, j: (i * tm // seq, 0, 0)),
                  pl.BlockSpec((d, IN_TN), lambda i, j: (0, j))],
        out_specs=pl.BlockSpec((tm, IN_TN), lambda i, j: (i, j)),
        out_shape=jax.ShapeDtypeStruct((t, n), F32),
        scratch_shapes=[pltpu.VMEM((tm, d), BF16)],
        compiler_params=_cparams("parallel", "arbitrary"),
        name="norm_inproj",
    )(x2, norm_w.reshape(1, d), scale, shift, w_packed)


def _pack_w_in(w):
    sizes = (FOX_WIDTH, FOX_WIDTH, FOX_WIDTH, FOX_HEADS, FOX_WIDTH,
             DSA_WIDTH, DSA_KV_WIDTH, DSA_KV_WIDTH,
             IDX_WIDTH, IDX_DIM, IDX_HEADS, DSA_WIDTH,
             SSD_WIDTH, CONV_CH, SSD_HEADS, N_BRANCH * w.shape[0])
    offs = np.concatenate([[0], np.cumsum(sizes)])
    (fq, fk, fv, ff, fg, dq, dk, dv, iq, ik, iw, dg, sz, sxbc, sdt, mg) = [
        w[:, int(offs[i]):int(offs[i + 1])].astype(BF16) for i in range(len(sizes))]
    d = w.shape[0]
    pad8 = jnp.zeros((d, LANES - (IDX_DIM + FOX_HEADS + IDX_HEADS + SSD_HEADS)), BF16)
    cols = [sxbc, fq, fk, fv, fg, dq, iq, dg, sz, mg, dk, dv, ik, ff, iw, sdt, pad8]
    packed = jnp.concatenate(cols, axis=1)
    pad = (-packed.shape[1]) % IN_TN
    return jnp.concatenate([packed, jnp.zeros((d, pad), BF16)], axis=1)


def _rope_tab_kernel(pos_ref, c128, sa128, sb128, c64, sa64, sb64, cik, saik, sbik):
    pos = pos_ref[...].astype(F32)
    lane = lax.broadcasted_iota(jnp.int32, (8, LANES), 1)[0:1, :]

    def tables(d):
        rd = d // ROPE_FRACTION
        half = rd // 2
        j = lane % d
        inv = jnp.power(ROPE_THETA, -(j % half).astype(F32) / half)
        ang = pos * inv
        cos = jnp.cos(ang)
        sin = jnp.sin(ang)
        c = jnp.where(j < rd, cos, 1.0)
        sa = jnp.where(j < half, -sin, 0.0)
        sb = jnp.where(j < half, 0.0, jnp.where(j < rd, sin, 0.0))
        return c, sa, sb

    c, sa, sb = tables(HEAD_DIM)
    c128[...] = c
    sa128[...] = sa
    sb128[...] = sb
    c, sa, sb = tables(IDX_DIM)
    c64[...] = c
    sa64[...] = sa
    sb64[...] = sb
    keep = lane < IDX_DIM
    cik[...] = jnp.where(keep, c, 0.0)
    saik[...] = jnp.where(keep, sa, 0.0)
    sbik[...] = jnp.where(keep, sb, 0.0)


def _rope_tables(pos2):
    t = pos2.shape[0]
    tp = min(PREP_TP, t)
    spec = pl.BlockSpec((tp, LANES), lambda i: (i, 0))
    return pl.pallas_call(
        _rope_tab_kernel,
        grid=(t // tp,),
        in_specs=[pl.BlockSpec((tp, 1), lambda i: (i, 0))],
        out_specs=[spec] * 9,
        out_shape=[jax.ShapeDtypeStruct((t, LANES), F32)] * 9,
        compiler_params=_cparams("parallel"),
        name="rope_tables",
    )(pos2)


def _fox_prep_kernel(q_ref, k_ref, v_ref, misc_ref, bias_ref, qw_ref, kw_ref,
                     qt_ref, ka_ref, vt_ref, carry_ref, *, tiles_per_batch):
    @pl.when(pl.program_id(0) % tiles_per_batch == 0)
    def _():
        carry_ref[...] = jnp.zeros_like(carry_ref)

    tp = q_ref.shape[0]
    cs = _cumsum_rows(_log_sigmoid(misc_ref[...] + bias_ref[...])) + carry_ref[0:1, :]
    carry_ref[...] = jnp.broadcast_to(cs[tp - 1:tp, :], carry_ref.shape)
    f = cs * LOG2E
    f1 = f.astype(BF16).astype(F32)
    r1 = f - f1
    f2 = r1.astype(BF16).astype(F32)
    f3 = r1 - f2
    cols = (f1, f2, f3)
    rows = tuple(x.T for x in cols)
    sub = lax.broadcasted_iota(jnp.int32, (8, tp), 0)
    lane = lax.broadcasted_iota(jnp.int32, (tp, LANES), 1)
    zeros_t = jnp.zeros((HEAD_DIM - 8, tp), F32)
    qw = qw_ref[...] * (ATT_SCALE * LOG2E)
    kw = kw_ref[...]
    for h in range(FOX_HEADS):
        hs = slice(h * HEAD_DIM, (h + 1) * HEAD_DIM)
        hl = MISC_FF + h
        qn = _rms(q_ref[:, hs], qw)
        aug = jnp.where(sub == 0, rows[0][hl:hl + 1, :],
                        jnp.where(sub == 1, rows[1][hl:hl + 1, :],
                                  jnp.where(sub == 2, rows[2][hl:hl + 1, :],
                                            jnp.where(sub < 6, 1.0, 0.0))))
        qt_ref[h * FOX_AUG:h * FOX_AUG + HEAD_DIM, :] = qn.T.astype(BF16)
        qt_ref[h * FOX_AUG + HEAD_DIM:(h + 1) * FOX_AUG, :] = jnp.concatenate(
            [aug, zeros_t], axis=0).astype(BF16)
        ka_ref[:, h * FOX_AUG:h * FOX_AUG + HEAD_DIM] = _rms(k_ref[:, hs], kw).astype(BF16)
        kaug = jnp.where(lane < 3, 1.0,
                         jnp.where(lane == 3, -cols[0][:, hl:hl + 1],
                                   jnp.where(lane == 4, -cols[1][:, hl:hl + 1],
                                             jnp.where(lane == 5, -cols[2][:, hl:hl + 1], 0.0))))
        ka_ref[:, h * FOX_AUG + HEAD_DIM:(h + 1) * FOX_AUG] = kaug.astype(BF16)
        vt_ref[hs, :] = v_ref[:, hs].T.astype(BF16)


def _fox_prep(proj, bias128, qw, kw, seq, misc_blk):
    t = proj.shape[0]
    tp = min(PREP_TP, seq)
    wide = lambda off: pl.BlockSpec((tp, FOX_WIDTH), lambda i: (i, off // FOX_WIDTH))
    row = pl.BlockSpec((1, LANES), lambda i: (0, 0))
    return pl.pallas_call(
        functools.partial(_fox_prep_kernel, tiles_per_batch=seq // tp),
        grid=(t // tp,),
        in_specs=[wide(OFF_FQ), wide(OFF_FK), wide(OFF_FV),
                  pl.BlockSpec((tp, LANES), lambda i: (i, misc_blk)), row, row, row],
        out_specs=[pl.BlockSpec((FOX_HEADS * FOX_AUG, tp), lambda i: (0, i)),
                   pl.BlockSpec((tp, FOX_HEADS * FOX_AUG), lambda i: (i, 0)),
                   pl.BlockSpec((FOX_WIDTH, tp), lambda i: (0, i))],
        out_shape=[jax.ShapeDtypeStruct((FOX_HEADS * FOX_AUG, t), BF16),
                   jax.ShapeDtypeStruct((t, FOX_HEADS * FOX_AUG), BF16),
                   jax.ShapeDtypeStruct((FOX_WIDTH, t), BF16)],
        scratch_shapes=[pltpu.VMEM((8, LANES), F32)],
        compiler_params=_cparams("arbitrary"),
        name="fox_prep",
    )(proj, proj, proj, proj, bias128, qw.reshape(1, LANES), kw.reshape(1, LANES))


def _fox_kernel(qt_ref, ka_ref, vt_ref, g_ref, o_ref, acc_ref, *, tq, tk):
    i = pl.program_id(2)
    qt = qt_ref[...]
    acc_ref[...] = jnp.zeros_like(acc_ref)

    def update(s, start, size, carry):
        m, l = carry
        m_new = jnp.maximum(m, jnp.max(s, axis=0, keepdims=True))
        alpha = jnp.exp2(m - m_new)
        p = jnp.exp2(s - m_new)
        l = alpha * l + jnp.sum(p, axis=0, keepdims=True)
        acc_ref[...] = alpha * acc_ref[...] + jnp.dot(
            vt_ref[:, pl.ds(start, size)], p.astype(BF16), preferred_element_type=F32)
        return m_new, l

    def full_tile(j, carry):
        start = pl.multiple_of(j * tk, tk)
        s = jnp.dot(ka_ref[pl.ds(start, tk), :], qt, preferred_element_type=F32)
        return update(s, start, tk, carry)

    per = tk // tq
    nfull = i // per
    rem = i - nfull * per
    init = (jnp.full((1, tq), NEG, F32), jnp.zeros((1, tq), F32))
    carry = lax.fori_loop(0, nfull, full_tile, init)
    start = pl.multiple_of(nfull * tk, tk)
    for r in range(per):
        @pl.when(rem == r)
        def _():
            size = (r + 1) * tq
            s = jnp.dot(ka_ref[pl.ds(start, size), :], qt, preferred_element_type=F32)
            krow = lax.broadcasted_iota(jnp.int32, (size, tq), 0)
            qcol = lax.broadcasted_iota(jnp.int32, (size, tq), 1) + r * tq
            _, l = update(jnp.where(krow <= qcol, s, NEG), start, size, carry)
            o_ref[...] = ((acc_ref[...] / l).T * _silu(g_ref[...])).astype(BF16)


def _fox_attention(qt, ka, vt, proj, batch, seq):
    t = ka.shape[0]
    tq = min(FOX_TQ, seq)
    tk = min(FOX_TK, seq)
    nq = seq // tq
    gblk = OFF_FG // LANES
    return pl.pallas_call(
        functools.partial(_fox_kernel, tq=tq, tk=tk),
        grid=(batch, FOX_HEADS, nq),
        in_specs=[pl.BlockSpec((FOX_AUG, tq), lambda b, h, i: (h, b * nq + i)),
                  pl.BlockSpec((seq, FOX_AUG), lambda b, h, i: (b, h)),
                  pl.BlockSpec((HEAD_DIM, seq), lambda b, h, i: (h, b)),
                  pl.BlockSpec((tq, LANES), lambda b, h, i: (b * nq + i, gblk + h))],
        out_specs=pl.BlockSpec((tq, LANES), lambda b, h, i: (b * nq + i, h)),
        out_shape=jax.ShapeDtypeStruct((t, FOX_WIDTH), BF16),
        scratch_shapes=[pltpu.VMEM((HEAD_DIM, tq), F32)],
        compiler_params=_cparams("parallel", "parallel", "arbitrary"),
        name="fox_attention",
    )(qt, ka, vt, proj)


def _dsa_prep_kernel(dq_ref, iq_ref, kv_ref, misc_ref, c128, sa128, sb128, c64, sa64, sb64,
                     cik, saik, sbik, qw_ref, kw_ref, dqt_ref, iqt_ref, dk_ref, dvt_ref, ki_ref):
    t128 = (c128[...], sa128[...], sb128[...])
    t64 = (c64[...], sa64[...], sb64[...])
    qw = qw_ref[...] * (ATT_SCALE * LOG2E)
    kw = kw_ref[...]
    for h in range(DSA_HEADS):
        hs = slice(h * HEAD_DIM, (h + 1) * HEAD_DIM)
        dqt_ref[hs, :] = _rope(_rms(dq_ref[:, hs], qw), *t128, ROPE_HALF).T.astype(BF16)
    for p in range(IDX_WIDTH // LANES):
        ps = slice(p * LANES, (p + 1) * LANES)
        iqt_ref[ps, :] = _rope(iq_ref[:, ps], *t64, IDX_ROPE_HALF).T.astype(BF16)
    for g in range(DSA_KV_HEADS):
        gs = slice(g * HEAD_DIM, (g + 1) * HEAD_DIM)
        dk_ref[:, gs] = _rope(_rms(kv_ref[:, gs], kw), *t128, ROPE_HALF).astype(BF16)
        dvt_ref[gs, :] = kv_ref[:, DSA_KV_WIDTH + g * HEAD_DIM:
                                DSA_KV_WIDTH + (g + 1) * HEAD_DIM].T.astype(BF16)
    ik = _rope(misc_ref[...], cik[...], saik[...], sbik[...], IDX_ROPE_HALF)
    ki_ref[...] = (ik + pltpu.roll(ik, LANES // 2, 1)).astype(BF16)


def _dsa_prep(proj, tabs, qw, kw, seq, off_dk, misc_blk):
    t = proj.shape[0]
    tp = min(PREP_TP, seq)
    tab = pl.BlockSpec((tp, LANES), lambda i: (i, 0))
    row = pl.BlockSpec((1, LANES), lambda i: (0, 0))
    return pl.pallas_call(
        _dsa_prep_kernel,
        grid=(t // tp,),
        in_specs=[pl.BlockSpec((tp, DSA_WIDTH), lambda i: (i, OFF_DQ // DSA_WIDTH)),
                  pl.BlockSpec((tp, IDX_WIDTH), lambda i: (i, OFF_IQ // IDX_WIDTH)),
                  pl.BlockSpec((tp, 2 * DSA_KV_WIDTH), lambda i: (i, off_dk // (2 * DSA_KV_WIDTH))),
                  pl.BlockSpec((tp, LANES), lambda i: (i, misc_blk))] + [tab] * 9 + [row, row],
        out_specs=[pl.BlockSpec((DSA_WIDTH, tp), lambda i: (0, i)),
                   pl.BlockSpec((IDX_WIDTH, tp), lambda i: (0, i)),
                   pl.BlockSpec((tp, DSA_KV_WIDTH), lambda i: (i, 0)),
                   pl.BlockSpec((DSA_KV_WIDTH, tp), lambda i: (0, i)),
                   pl.BlockSpec((tp, LANES), lambda i: (i, 0))],
        out_shape=[jax.ShapeDtypeStruct((DSA_WIDTH, t), BF16),
                   jax.ShapeDtypeStruct((IDX_WIDTH, t), BF16),
                   jax.ShapeDtypeStruct((t, DSA_KV_WIDTH), BF16),
                   jax.ShapeDtypeStruct((DSA_KV_WIDTH, t), BF16),
                   jax.ShapeDtypeStruct((t, LANES), BF16)],
        compiler_params=_cparams("parallel"),
        name="dsa_prep",
    )(proj, proj, proj, proj, *tabs, qw.reshape(1, LANES), kw.reshape(1, LANES))


def _dsa_kernel(dqt_ref, dk_ref, dvt_ref, iqt_ref, ki_ref, misc_ref, g_ref, o_ref,
                keys_ref, qm_ref, acc_ref, *, tq, tk, topk):
    i = pl.program_id(1)
    q0 = i * tq
    nkt = (q0 + tq - 1) // tk + 1
    krow = lax.broadcasted_iota(jnp.int32, (tk, tq), 0)
    qcol = q0 + lax.broadcasted_iota(jnp.int32, (tk, tq), 1)
    w_t = (misc_ref[...] * IDX_SCALE).T

    zero = jnp.zeros((LANES - IDX_DIM, tq), BF16)
    for hh in range(IDX_HEADS):
        qm_ref[hh] = jnp.concatenate([iqt_ref[hh * IDX_DIM:(hh + 1) * IDX_DIM, :], zero], axis=0)

    def score_tile(j, carry):
        start = pl.multiple_of(j * tk, tk)
        kk = ki_ref[pl.ds(start, tk), :]
        acc = jnp.zeros((tk, tq), F32)
        for hh in range(IDX_HEADS):
            s = jnp.dot(kk, qm_ref[hh], preferred_element_type=F32)
            acc = acc + w_t[MISC_IW + hh:MISC_IW + hh + 1, :] * jnp.maximum(s, 0.0)
        bits = pltpu.bitcast(acc, jnp.int32)
        key = bits ^ ((bits >> 31) & 0x7FFFFFFF)
        keys_ref[pl.ds(start, tk), :] = jnp.where(start + krow <= qcol, key, INT_MIN)
        return carry

    lax.fori_loop(0, nkt, score_tile, 0)

    def count_ge(cand):
        def body(j, c):
            start = pl.multiple_of(j * tk, tk)
            ge = jnp.where(keys_ref[pl.ds(start, tk), :] >= cand, 1.0, 0.0)
            return c + jnp.sum(ge.reshape(tk // CNT_ROWS, CNT_ROWS, tq), axis=0)
        c = lax.fori_loop(0, nkt, body, jnp.zeros((CNT_ROWS, tq), F32))
        return jnp.sum(c, axis=0, keepdims=True)

    def search(p, thr):
        cand = thr + jnp.left_shift(jnp.int32(1), 31 - p)
        return jnp.where(count_ge(cand) >= topk, cand, thr)

    thr = lax.fori_loop(0, 32, search, jnp.full((1, tq), INT_MIN, jnp.int32))

    cnt_ge = count_ge(thr)
    cnt_gt = count_ge(thr + 1)
    tied = jnp.where((cnt_ge > topk) & (thr > INT_MIN), 1.0, 0.0)

    @pl.when(jnp.max(tied) > 0.0)
    def _():
        quota = topk - cnt_gt
        ur = lax.broadcasted_iota(jnp.int32, (tk, tk), 0)
        uc = lax.broadcasted_iota(jnp.int32, (tk, tk), 1)
        tril = jnp.where(uc <= ur, 1.0, 0.0).astype(BF16)

        def fix(j, seen):
            start = pl.multiple_of(j * tk, tk)
            k = keys_ref[pl.ds(start, tk), :]
            eq = jnp.where(k == thr, 1.0, 0.0)
            rank = seen + jnp.dot(tril, eq.astype(BF16), preferred_element_type=F32)
            drop = (eq * tied) * jnp.where(rank > quota, 1.0, 0.0)
            keys_ref[pl.ds(start, tk), :] = jnp.where(drop > 0.0, thr - 1, k)
            return seen + jnp.sum(eq, axis=0, keepdims=True)

        lax.fori_loop(0, nkt, fix, jnp.zeros((1, tq), F32))

    thr = jnp.maximum(thr, INT_MIN + 1)

    rep = DSA_HEADS // DSA_KV_HEADS
    qs = [jnp.concatenate([dqt_ref[(g * rep + r) * HEAD_DIM:(g * rep + r + 1) * HEAD_DIM, :]
                           for r in range(rep)], axis=1) for g in range(DSA_KV_HEADS)]
    acc_ref[...] = jnp.zeros_like(acc_ref)

    def attend(j, carry):
        start = pl.multiple_of(j * tk, tk)
        nb = jnp.where(keys_ref[pl.ds(start, tk), :] >= thr, 0.0, NEG)
        nb = jnp.concatenate([nb] * rep, axis=1)
        out = []
        for g in range(DSA_KV_HEADS):
            m, l = carry[2 * g:2 * g + 2]
            gs = slice(g * HEAD_DIM, (g + 1) * HEAD_DIM)
            s = jnp.dot(dk_ref[pl.ds(start, tk), gs], qs[g], preferred_element_type=F32) + nb
            m_new = jnp.maximum(m, jnp.max(s, axis=0, keepdims=True))
            alpha = jnp.exp2(m - m_new)
            p = jnp.exp2(s - m_new)
            l = alpha * l + jnp.sum(p, axis=0, keepdims=True)
            acc_ref[g] = alpha * acc_ref[g] + jnp.dot(
                dvt_ref[gs, pl.ds(start, tk)], p.astype(BF16), preferred_element_type=F32)
            out.extend([m_new, l])
        return tuple(out)

    init = (jnp.full((1, rep * tq), M_INIT, F32), jnp.zeros((1, rep * tq), F32)) * DSA_KV_HEADS
    res = lax.fori_loop(0, nkt, attend, init)
    for g in range(DSA_KV_HEADS):
        o = acc_ref[g] / res[2 * g + 1]
        for r in range(rep):
            sl = slice((g * rep + r) * HEAD_DIM, (g * rep + r + 1) * HEAD_DIM)
            o_ref[:, sl] = (o[:, r * tq:(r + 1) * tq].T * _silu(g_ref[:, sl])).astype(BF16)


def _dsa_attention(dqt, dk, dvt, iqt, ki, proj, batch, seq, misc_blk):
    t = dk.shape[0]
    tq = min(DSA_TQ, seq)
    tk = min(ATT_TK, seq)
    nq = seq // tq
    topk = min(TOPK_MAX, seq // 4)
    rep = DSA_HEADS // DSA_KV_HEADS
    return pl.pallas_call(
        functools.partial(_dsa_kernel, tq=tq, tk=tk, topk=topk),
        grid=(batch, nq),
        in_specs=[pl.BlockSpec((DSA_WIDTH, tq), lambda b, i: (0, b * nq + i)),
                  pl.BlockSpec((seq, DSA_KV_WIDTH), lambda b, i: (b, 0)),
                  pl.BlockSpec((DSA_KV_WIDTH, seq), lambda b, i: (0, b)),
                  pl.BlockSpec((IDX_WIDTH, tq), lambda b, i: (0, b * nq + i)),
                  pl.BlockSpec((seq, LANES), lambda b, i: (b, 0)),
                  pl.BlockSpec((tq, LANES), lambda b, i: (b * nq + i, misc_blk)),
                  pl.BlockSpec((tq, DSA_WIDTH), lambda b, i: (b * nq + i, OFF_DG // DSA_WIDTH))],
        out_specs=pl.BlockSpec((tq, DSA_WIDTH), lambda b, i: (b * nq + i, 0)),
        out_shape=jax.ShapeDtypeStruct((t, DSA_WIDTH), BF16),
        scratch_shapes=[pltpu.VMEM((seq, tq), jnp.int32),
                        pltpu.VMEM((IDX_HEADS, LANES, tq), BF16),
                        pltpu.VMEM((DSA_KV_HEADS, HEAD_DIM, rep * tq), F32)],
        compiler_params=_cparams("parallel", "arbitrary"),
        name="dsa_attention",
    )(dqt, dk, dvt, iqt, ki, proj, proj)


def _ssd_kernel(xin_ref, z_ref, misc_ref, cw_ref, cb_ref, dtb_ref, alog_ref, dsk_ref, nw_ref,
                e_ref, o_ref, xbuf, xbc, ybuf, state, *, L):
    @pl.when(pl.program_id(1) == 0)
    def _():
        xbuf[0:8, :] = jnp.zeros((8, CONV_CH), F32)
        state[...] = jnp.zeros_like(state)

    xbuf[8:8 + L, :] = xin_ref[...]
    lc = 512
    for ch in range(CONV_CH // lc):
        ls = slice(ch * lc, (ch + 1) * lc)
        acc = cb_ref[:, ls] + cw_ref[3:4, ls] * xbuf[8:8 + L, ls]
        acc = acc + cw_ref[2:3, ls] * xbuf[7:7 + L, ls]
        acc = acc + cw_ref[1:2, ls] * xbuf[6:6 + L, ls]
        acc = acc + cw_ref[0:1, ls] * xbuf[5:5 + L, ls]
        xbc[:, ls] = _silu(acc)
    xbuf[0:8, :] = xbuf[L:L + 8, :]

    dt = _softplus(misc_ref[...] + dtb_ref[...])
    a = -jnp.exp(alog_ref[...])
    acs = _cumsum_rows(dt * a)
    ea = jnp.exp(acs)
    dte = jnp.exp(acs[L - 1:L, :] - acs)
    full = _dot_exact_rhs(jnp.concatenate([dt, ea, dte], axis=0), e_ref[...])
    dt_f = full[0:L]
    ea_f = full[L:2 * L]
    dte_f = full[2 * L:3 * L]
    acs_t = acs.T
    lane = lax.broadcasted_iota(jnp.int32, (L, LANES), 1)
    r = lax.broadcasted_iota(jnp.int32, (L, L), 0)
    c = lax.broadcasted_iota(jnp.int32, (L, L), 1)
    causal = c <= r

    gw = SSD_WIDTH // SSD_GROUPS
    hpg = SSD_HEADS // SSD_GROUPS
    for g in range(SSD_GROUPS):
        gs = slice(g * gw, (g + 1) * gw)
        xd = xbc[:, gs] * dt_f[:, gs]
        xd_b = xd.astype(BF16)
        bg = xbc[:, SSD_WIDTH + g * SSD_STATE:SSD_WIDTH + (g + 1) * SSD_STATE]
        cg = xbc[:, SSD_WIDTH + (SSD_GROUPS + g) * SSD_STATE:
                 SSD_WIDTH + (SSD_GROUPS + g + 1) * SSD_STATE].astype(BF16)
        cbm = lax.dot_general(cg, bg.astype(BF16), (((1,), (1,)), ((), ())),
                              preferred_element_type=F32)
        for pr in range(hpg // 2):
            xpair = xd_b[:, pr * LANES:(pr + 1) * LANES]
            ys = []
            for hh in range(2):
                hl = MISC_DT + g * hpg + 2 * pr + hh
                seg = jnp.where(causal, acs[:, hl:hl + 1] - acs_t[hl:hl + 1, :], NEG)
                mm = (cbm * jnp.exp(seg)).astype(BF16)
                ys.append(jnp.dot(mm, xpair, preferred_element_type=F32))
            ybuf[:, g * gw + pr * LANES:g * gw + (pr + 1) * LANES] = jnp.where(
                lane < SSD_HEAD_DIM, ys[0], ys[1])
        st = state[:, gs]
        yoff = jnp.dot(cg, st.astype(BF16), preferred_element_type=F32)
        ybuf[:, gs] = ybuf[:, gs] + yoff * ea_f[:, gs]
        zz = (xd * dte_f[:, gs]).astype(BF16)
        state[:, gs] = st * ea_f[L - 1:L, gs] + jnp.dot(bg.T.astype(BF16), zz,
                                                        preferred_element_type=F32)
        y = ybuf[:, gs] + dsk_ref[:, gs] * xbc[:, gs]
        y = y * _silu(z_ref[:, gs])
        o_ref[:, gs] = _rms(y, nw_ref[:, gs]).astype(BF16)


def _ssd(proj, conv_w, conv_b, dtb128, alog128, dskip_f, ssd_norm, expand, batch, seq, misc_blk):
    t = proj.shape[0]
    L = SSD_CHUNK
    nc = seq // L
    const = lambda shape: pl.BlockSpec(shape, lambda b, c: (0, 0))
    return pl.pallas_call(
        functools.partial(_ssd_kernel, L=L),
        grid=(batch, nc),
        in_specs=[pl.BlockSpec((L, CONV_CH), lambda b, c: (b * nc + c, 0)),
                  pl.BlockSpec((L, SSD_WIDTH), lambda b, c: (b * nc + c, OFF_SZ // SSD_WIDTH)),
                  pl.BlockSpec((L, LANES), lambda b, c: (b * nc + c, misc_blk)),
                  const((CONV_WIDTH, CONV_CH)), const((1, CONV_CH)), const((1, LANES)),
                  const((1, LANES)), const((1, SSD_WIDTH)), const((1, SSD_WIDTH)),
                  const((LANES, SSD_WIDTH))],
        out_specs=pl.BlockSpec((L, SSD_WIDTH), lambda b, c: (b * nc + c, 0)),
        out_shape=jax.ShapeDtypeStruct((t, SSD_WIDTH), BF16),
        scratch_shapes=[pltpu.VMEM((L + 8, CONV_CH), F32), pltpu.VMEM((L, CONV_CH), F32),
                        pltpu.VMEM((L, SSD_WIDTH), F32), pltpu.VMEM((SSD_STATE, SSD_WIDTH), F32)],
        compiler_params=_cparams("parallel", "arbitrary"),
        name="ssd_scan",
    )(proj, proj, proj, conv_w, conv_b.reshape(1, CONV_CH), dtb128, alog128, dskip_f,
      ssd_norm.reshape(1, SSD_WIDTH), expand)


def _merge_kernel(yf_ref, yd_ref, ys_ref, wf_ref, wd_ref, ws_ref, g0_ref, g1_ref, g2_ref,
                  bm_ref, o_ref):
    a = jnp.dot(yf_ref[...], wf_ref[...], preferred_element_type=F32)
    b = jnp.dot(yd_ref[...], wd_ref[...], preferred_element_type=F32)
    c = jnp.dot(ys_ref[...], ws_ref[...], preferred_element_type=F32)
    m = (jax.nn.sigmoid(g0_ref[...] + bm_ref[0:1, :]) * a
         + jax.nn.sigmoid(g1_ref[...] + bm_ref[1:2, :]) * b
         + jax.nn.sigmoid(g2_ref[...] + bm_ref[2:3, :]) * c)
    o_ref[...] = m.astype(BF16)


def _merge(y_fox, y_dsa, y_ssd, wf, wd, ws, proj, b_merge):
    t = y_fox.shape[0]
    d = wf.shape[1]
    tm = OUT_TM
    tn = OUT_TN
    gb = OFF_MG // tn
    gspec = lambda br: pl.BlockSpec((tm, tn), lambda j, i: (i, gb + br * (d // tn) + j))
    return pl.pallas_call(
        _merge_kernel,
        grid=(d // tn, t // tm),
        in_specs=[pl.BlockSpec((tm, FOX_WIDTH), lambda j, i: (i, 0)),
                  pl.BlockSpec((tm, DSA_WIDTH), lambda j, i: (i, 0)),
                  pl.BlockSpec((tm, SSD_WIDTH), lambda j, i: (i, 0)),
                  pl.BlockSpec((FOX_WIDTH, tn), lambda j, i: (0, j)),
                  pl.BlockSpec((DSA_WIDTH, tn), lambda j, i: (0, j)),
                  pl.BlockSpec((SSD_WIDTH, tn), lambda j, i: (0, j)),
                  gspec(0), gspec(1), gspec(2),
                  pl.BlockSpec((N_BRANCH, tn), lambda j, i: (0, j))],
        out_specs=pl.BlockSpec((tm, tn), lambda j, i: (i, j)),
        out_shape=jax.ShapeDtypeStruct((t, d), BF16),
        compiler_params=_cparams("parallel", "arbitrary"),
        name="branch_merge",
    )(y_fox, y_dsa, y_ssd, wf, wd, ws, proj, proj, proj, b_merge)


def _outproj_kernel(m_ref, w_ref, x_ref, g_ref, o_ref):
    out = jnp.dot(m_ref[...], w_ref[...], preferred_element_type=F32)
    o_ref[...] = x_ref[...] + g_ref[...] * out


def _outproj(merged, w_out, x2, gate, seq):
    t, d = x2.shape
    tm = OUT_TM
    tn = OUT_TN
    return pl.pallas_call(
        _outproj_kernel,
        grid=(d // tn, t // tm),
        in_specs=[pl.BlockSpec((tm, d), lambda j, i: (i, 0)),
                  pl.BlockSpec((d, tn), lambda j, i: (0, j)),
                  pl.BlockSpec((tm, tn), lambda j, i: (i, j)),
                  pl.BlockSpec((None, 1, tn), lambda j, i: (i * tm // seq, 0, j))],
        out_specs=pl.BlockSpec((tm, tn), lambda j, i: (i, j)),
        out_shape=jax.ShapeDtypeStruct((t, d), F32),
        compiler_params=_cparams("parallel", "arbitrary"),
        name="out_proj_residual",
    )(merged, w_out, x2, gate)


def _lane_block(vec, off):
    return jnp.zeros((1, LANES), F32).at[0, off:off + vec.shape[0]].set(vec.astype(F32))


def kernel(x, c, positions, norm_w, w_ada, b_ada, w_in, b_fox_f, fox_q_norm, fox_k_norm, dsa_q_norm, dsa_k_norm, conv_w, conv_b, dt_bias, a_log, d_skip, ssd_norm, b_merge, w_o_fox, w_o_dsa, w_o_ssd, w_out):
    batch, seq, d = x.shape
    depth = w_in.shape[0]
    t = batch * seq
    assert seq % 512 == 0 and d % 512 == 0

    c8 = jnp.zeros((8, d), F32).at[:batch].set(c)
    mod = _ada(c8, w_ada, b_ada)
    tabs = _rope_tables(positions.reshape(t, 1))

    expand = np.zeros((LANES, SSD_WIDTH), np.float32)
    for hh in range(SSD_HEADS):
        expand[MISC_DT + hh, hh * SSD_HEAD_DIM:(hh + 1) * SSD_HEAD_DIM] = 1.0
    expand = jnp.asarray(expand, BF16)

    off_dk = OFF_MG + N_BRANCH * d
    misc_blk = (off_dk + 2 * DSA_KV_WIDTH) // LANES

    h = x.reshape(t, d)
    for l in range(depth):
        shift = mod[l, :batch, 0:d].reshape(batch, 1, d)
        scale = mod[l, :batch, d:2 * d].reshape(batch, 1, d)
        gate = mod[l, :batch, 2 * d:3 * d].reshape(batch, 1, d)
        proj = _inproj(h, norm_w[l], scale, shift, _pack_w_in(w_in[l]), seq)

        fqt, fka, fvt = _fox_prep(proj, _lane_block(b_fox_f[l], MISC_FF), fox_q_norm[l],
                                  fox_k_norm[l], seq, misc_blk)
        y_fox = _fox_attention(fqt, fka, fvt, proj, batch, seq)

        dqt, iqt, dk, dvt, ki = _dsa_prep(proj, tabs, dsa_q_norm[l], dsa_k_norm[l], seq,
                                          off_dk, misc_blk)
        y_dsa = _dsa_attention(dqt, dk, dvt, iqt, ki, proj, batch, seq, misc_blk)

        y_ssd = _ssd(proj, conv_w[l], conv_b[l], _lane_block(dt_bias[l], MISC_DT),
                     _lane_block(a_log[l], MISC_DT),
                     jnp.repeat(d_skip[l].astype(F32), SSD_HEAD_DIM).reshape(1, SSD_WIDTH),
                     ssd_norm[l], expand, batch, seq, misc_blk)

        merged = _merge(y_fox, y_dsa, y_ssd, w_o_fox[l].astype(BF16), w_o_dsa[l].astype(BF16),
                        w_o_ssd[l].astype(BF16), proj, b_merge[l])
        h = _outproj(merged, w_out[l].astype(BF16), h, gate, seq)
    return h.reshape(batch, seq, d)
```

```python
import functools
import math

import numpy as np
import jax
import jax.numpy as jnp
from jax import lax
from jax.experimental import pallas as pl
from jax.experimental.pallas import tpu as pltpu

F32 = jnp.float32
BF16 = jnp.bfloat16
HIGHEST = lax.Precision.HIGHEST

HEAD_DIM = 128
FOX_HEADS = 8
FOX_WIDTH = FOX_HEADS * HEAD_DIM
DSA_HEADS = 8
DSA_KV_HEADS = 2
DSA_WIDTH = DSA_HEADS * HEAD_DIM
DSA_KV_WIDTH = DSA_KV_HEADS * HEAD_DIM
IDX_HEADS = 16
IDX_DIM = 64
IDX_WIDTH = IDX_HEADS * IDX_DIM
TOPK_MAX = 256
SSD_HEADS = 32
SSD_HEAD_DIM = 64
SSD_WIDTH = SSD_HEADS * SSD_HEAD_DIM
SSD_GROUPS = 4
SSD_STATE = 128
SSD_CHUNK = 128
CONV_WIDTH = 4
CONV_CH = SSD_WIDTH + 2 * SSD_GROUPS * SSD_STATE
N_BRANCH = 3
ROPE_THETA = 500000.0
ROPE_FRACTION = 4
EPS = 1e-6

LANES = 128
NEG = -1e30
M_INIT = -1e29
INT_MIN = -(2 ** 31)
VMEM_LIMIT = 52 * 1024 * 1024
LOG2E = math.log2(math.e)
ATT_SCALE = HEAD_DIM ** -0.5
IDX_SCALE = (IDX_DIM ** -0.5) * (IDX_HEADS ** -0.5)
ROPE_HALF = HEAD_DIM // ROPE_FRACTION // 2
IDX_ROPE_HALF = IDX_DIM // ROPE_FRACTION // 2
FOX_AUG = 2 * HEAD_DIM

OFF_XBC = 0
OFF_FQ = OFF_XBC + CONV_CH
OFF_FK = OFF_FQ + FOX_WIDTH
OFF_FV = OFF_FK + FOX_WIDTH
OFF_FG = OFF_FV + FOX_WIDTH
OFF_DQ = OFF_FG + FOX_WIDTH
OFF_IQ = OFF_DQ + DSA_WIDTH
OFF_DG = OFF_IQ + IDX_WIDTH
OFF_SZ = OFF_DG + DSA_WIDTH
OFF_MG = OFF_SZ + SSD_WIDTH
MISC_IK = 0
MISC_FF = MISC_IK + IDX_DIM
MISC_IW = MISC_FF + FOX_HEADS
MISC_DT = MISC_IW + IDX_HEADS
IN_TN = 1280
OUT_TM = 512
OUT_TN = 1024
PREP_TP = 512
ATT_TK = 512
FOX_TQ = 512
FOX_TK = 4096
DSA_TQ = 256
CNT_ROWS = 64


def _cparams(*sem):
    return pltpu.CompilerParams(dimension_semantics=sem, vmem_limit_bytes=VMEM_LIMIT)


def _silu(x):
    return x * jax.nn.sigmoid(x)


def _softplus(x):
    return jnp.maximum(x, 0.0) + jnp.log1p(jnp.exp(-jnp.abs(x)))


def _log_sigmoid(x):
    return jnp.minimum(x, 0.0) - jnp.log1p(jnp.exp(-jnp.abs(x)))


def _rms(x, w):
    return x * lax.rsqrt(jnp.mean(x * x, axis=-1, keepdims=True) + EPS) * w


def _split3(x):
    x1 = x.astype(BF16)
    r = x - x1.astype(F32)
    x2 = r.astype(BF16)
    x3 = (r - x2.astype(F32)).astype(BF16)
    return x1, x2, x3


def _dot_exact_rhs(x, m_bf16):
    x1, x2, x3 = _split3(x)
    return (jnp.dot(x1, m_bf16, preferred_element_type=F32)
            + jnp.dot(x2, m_bf16, preferred_element_type=F32)
            + jnp.dot(x3, m_bf16, preferred_element_type=F32))


def _cumsum_rows(x):
    n = x.shape[0]
    r = lax.broadcasted_iota(jnp.int32, (n, n), 0)
    c = lax.broadcasted_iota(jnp.int32, (n, n), 1)
    tril = jnp.where(c <= r, 1.0, 0.0).astype(BF16)
    x1, x2, x3 = _split3(x)
    return (jnp.dot(tril, x1, preferred_element_type=F32)
            + jnp.dot(tril, x2, preferred_element_type=F32)
            + jnp.dot(tril, x3, preferred_element_type=F32))


def _rope(x, c, sa, sb, half):
    return x * c + pltpu.roll(x, LANES - half, 1) * sa + pltpu.roll(x, half, 1) * sb


def _ada_kernel(c_ref, w_ref, b_ref, o_ref):
    s = _silu(c_ref[...])
    o_ref[0] = jnp.dot(s, w_ref[0], precision=HIGHEST, preferred_element_type=F32) + b_ref[0]


def _ada(c8, w_ada, b_ada):
    depth, d, n = w_ada.shape
    tn = 512
    return pl.pallas_call(
        _ada_kernel,
        grid=(depth, n // tn),
        in_specs=[pl.BlockSpec((8, d), lambda l, j: (0, 0)),
                  pl.BlockSpec((1, d, tn), lambda l, j: (l, 0, j)),
                  pl.BlockSpec((1, 1, tn), lambda l, j: (l, 0, j))],
        out_specs=pl.BlockSpec((1, 8, tn), lambda l, j: (l, 0, j)),
        out_shape=jax.ShapeDtypeStruct((depth, 8, n), F32),
        compiler_params=_cparams("parallel", "parallel"),
        name="ada_mod",
    )(c8, w_ada, b_ada.reshape(depth, 1, n))


def _inproj_kernel(x_ref, nw_ref, sc_ref, sh_ref, w_ref, o_ref, u_ref, *, rows):
    @pl.when(pl.program_id(1) == 0)
    def _():
        nw = nw_ref[...]
        sc = 1.0 + sc_ref[...]
        sh = sh_ref[...]

        def body(r, carry):
            sl = pl.ds(pl.multiple_of(r * rows, rows), rows)
            u_ref[sl, :] = (_rms(x_ref[sl, :], nw) * sc + sh).astype(BF16)
            return carry

        lax.fori_loop(0, x_ref.shape[0] // rows, body, 0)

    o_ref[...] = jnp.dot(u_ref[...], w_ref[...], preferred_element_type=F32)


def _inproj(x2, norm_w, scale, shift, w_packed, seq):
    t, d = x2.shape
    n = w_packed.shape[1]
    tm = min(1024, seq)
    return pl.pallas_call(
        functools.partial(_inproj_kernel, rows=64),
        grid=(t // tm, n // IN_TN),
        in_specs=[pl.BlockSpec((tm, d), lambda i, j: (i, 0)),
                  pl.BlockSpec((1, d), lambda i, j: (0, 0)),
                  pl.BlockSpec((None, 1, d), lambda i, j: (i * tm // seq, 0, 0)),
                  pl.BlockSpec((None, 1, d), lambda i, j: (i * tm // seq, 0, 0)),
                  pl.BlockSpec((d, IN_TN), lambda i, j: (0, j))],
        out_specs=pl.BlockSpec((tm, IN_TN), lambda i, j: (i, j)),
        out_shape=jax.ShapeDtypeStruct((t, n), F32),
        scratch_shapes=[pltpu.VMEM((tm, d), BF16)],
        compiler_params=_cparams("parallel", "arbitrary"),
        name="norm_inproj",
    )(x2, norm_w.reshape(1, d), scale, shift, w_packed)


def _pack_w_in(w):
    sizes = (FOX_WIDTH, FOX_WIDTH, FOX_WIDTH, FOX_HEADS, FOX_WIDTH,
             DSA_WIDTH, DSA_KV_WIDTH, DSA_KV_WIDTH,
             IDX_WIDTH, IDX_DIM, IDX_HEADS, DSA_WIDTH,
             SSD_WIDTH, CONV_CH, SSD_HEADS, N_BRANCH * w.shape[0])
    offs = np.concatenate([[0], np.cumsum(sizes)])
    (fq, fk, fv, ff, fg, dq, dk, dv, iq, ik, iw, dg, sz, sxbc, sdt, mg) = [
        w[:, int(offs[i]):int(offs[i + 1])].astype(BF16) for i in range(len(sizes))]
    d = w.shape[0]
    pad8 = jnp.zeros((d, LANES - (IDX_DIM + FOX_HEADS + IDX_HEADS + SSD_HEADS)), BF16)
    cols = [sxbc, fq, fk, fv, fg, dq, iq, dg, sz, mg, dk, dv, ik, ff, iw, sdt, pad8]
    packed = jnp.concatenate(cols, axis=1)
    pad = (-packed.shape[1]) % IN_TN
    return jnp.concatenate([packed, jnp.zeros((d, pad), BF16)], axis=1)


def _rope_tab_kernel(pos_ref, c128, sa128, sb128, c64, sa64, sb64, cik, saik, sbik):
    pos = pos_ref[...].astype(F32)
    lane = lax.broadcasted_iota(jnp.int32, (8, LANES), 1)[0:1, :]

    def tables(d):
        rd = d // ROPE_FRACTION
        half = rd // 2
        j = lane % d
        inv = jnp.power(ROPE_THETA, -(j % half).astype(F32) / half)
        ang = pos * inv
        cos = jnp.cos(ang)
        sin = jnp.sin(ang)
        c = jnp.where(j < rd, cos, 1.0)
        sa = jnp.where(j < half, -sin, 0.0)
        sb = jnp.where(j < half, 0.0, jnp.where(j < rd, sin, 0.0))
        return c, sa, sb

    c, sa, sb = tables(HEAD_DIM)
    c128[...] = c
    sa128[...] = sa
    sb128[...] = sb
    c, sa, sb = tables(IDX_DIM)
    c64[...] = c
    sa64[...] = sa
    sb64[...] = sb
    keep = lane < IDX_DIM
    cik[...] = jnp.where(keep, c, 0.0)
    saik[...] = jnp.where(keep, sa, 0.0)
    sbik[...] = jnp.where(keep, sb, 0.0)


def _rope_tables(pos2):
    t = pos2.shape[0]
    tp = min(PREP_TP, t)
    spec = pl.BlockSpec((tp, LANES), lambda i: (i, 0))
    return pl.pallas_call(
        _rope_tab_kernel,
        grid=(t // tp,),
        in_specs=[pl.BlockSpec((tp, 1), lambda i: (i, 0))],
        out_specs=[spec] * 9,
        out_shape=[jax.ShapeDtypeStruct((t, LANES), F32)] * 9,
        compiler_params=_cparams("parallel"),
        name="rope_tables",
    )(pos2)


def _fox_prep_kernel(q_ref, k_ref, v_ref, misc_ref, bias_ref, qw_ref, kw_ref,
                     qt_ref, ka_ref, vt_ref, carry_ref, *, tiles_per_batch):
    @pl.when(pl.program_id(0) % tiles_per_batch == 0)
    def _():
        carry_ref[...] = jnp.zeros_like(carry_ref)

    tp = q_ref.shape[0]
    cs = _cumsum_rows(_log_sigmoid(misc_ref[...] + bias_ref[...])) + carry_ref[0:1, :]
    carry_ref[...] = jnp.broadcast_to(cs[tp - 1:tp, :], carry_ref.shape)
    f = cs * LOG2E
    f1 = f.astype(BF16).astype(F32)
    r1 = f - f1
    f2 = r1.astype(BF16).astype(F32)
    f3 = r1 - f2
    cols = (f1, f2, f3)
    rows = tuple(x.T for x in cols)
    sub = lax.broadcasted_iota(jnp.int32, (8, tp), 0)
    lane = lax.broadcasted_iota(jnp.int32, (tp, LANES), 1)
    zeros_t = jnp.zeros((HEAD_DIM - 8, tp), F32)
    qw = qw_ref[...] * (ATT_SCALE * LOG2E)
    kw = kw_ref[...]
    for h in range(FOX_HEADS):
        hs = slice(h * HEAD_DIM, (h + 1) * HEAD_DIM)
        hl = MISC_FF + h
        qn = _rms(q_ref[:, hs], qw)
        aug = jnp.where(sub == 0, rows[0][hl:hl + 1, :],
                        jnp.where(sub == 1, rows[1][hl:hl + 1, :],
                                  jnp.where(sub == 2, rows[2][hl:hl + 1, :],
                                            jnp.where(sub < 6, 1.0, 0.0))))
        qt_ref[h * FOX_AUG:h * FOX_AUG + HEAD_DIM, :] = qn.T.astype(BF16)
        qt_ref[h * FOX_AUG + HEAD_DIM:(h + 1) * FOX_AUG, :] = jnp.concatenate(
            [aug, zeros_t], axis=0).astype(BF16)
        ka_ref[:, h * FOX_AUG:h * FOX_AUG + HEAD_DIM] = _rms(k_ref[:, hs], kw).astype(BF16)
        kaug = jnp.where(lane < 3, 1.0,
                         jnp.where(lane == 3, -cols[0][:, hl:hl + 1],
                                   jnp.where(lane == 4, -cols[1][:, hl:hl + 1],
                                             jnp.where(lane == 5, -cols[2][:, hl:hl + 1], 0.0))))
        ka_ref[:, h * FOX_AUG + HEAD_DIM:(h + 1) * FOX_AUG] = kaug.astype(BF16)
        vt_ref[hs, :] = v_ref[:, hs].T.astype(BF16)


def _fox_prep(proj, bias128, qw, kw, seq, misc_blk):
    t = proj.shape[0]
    tp = min(PREP_TP, seq)
    wide = lambda off: pl.BlockSpec((tp, FOX_WIDTH), lambda i: (i, off // FOX_WIDTH))
    row = pl.BlockSpec((1, LANES), lambda i: (0, 0))
    return pl.pallas_call(
        functools.partial(_fox_prep_kernel, tiles_per_batch=seq // tp),
        grid=(t // tp,),
        in_specs=[wide(OFF_FQ), wide(OFF_FK), wide(OFF_FV),
                  pl.BlockSpec((tp, LANES), lambda i: (i, misc_blk)), row, row, row],
        out_specs=[pl.BlockSpec((FOX_HEADS * FOX_AUG, tp), lambda i: (0, i)),
                   pl.BlockSpec((tp, FOX_HEADS * FOX_AUG), lambda i: (i, 0)),
                   pl.BlockSpec((FOX_WIDTH, tp), lambda i: (0, i))],
        out_shape=[jax.ShapeDtypeStruct((FOX_HEADS * FOX_AUG, t), BF16),
                   jax.ShapeDtypeStruct((t, FOX_HEADS * FOX_AUG), BF16),
                   jax.ShapeDtypeStruct((FOX_WIDTH, t), BF16)],
        scratch_shapes=[pltpu.VMEM((8, LANES), F32)],
        compiler_params=_cparams("arbitrary"),
        name="fox_prep",
    )(proj, proj, proj, proj, bias128, qw.reshape(1, LANES), kw.reshape(1, LANES))


def _fox_kernel(qt_ref, ka_ref, vt_ref, g_ref, o_ref, acc_ref, *, tq, tk):
    i = pl.program_id(2)
    qt = qt_ref[...]
    acc_ref[...] = jnp.zeros_like(acc_ref)

    def update(s, start, size, carry):
        m, l = carry
        m_new = jnp.maximum(m, jnp.max(s, axis=0, keepdims=True))
        alpha = jnp.exp2(m - m_new)
        p = jnp.exp2(s - m_new)
        l = alpha * l + jnp.sum(p, axis=0, keepdims=True)
        acc_ref[...] = alpha * acc_ref[...] + jnp.dot(
            vt_ref[:, pl.ds(start, size)], p.astype(BF16), preferred_element_type=F32)
        return m_new, l

    def full_tile(j, carry):
        start = pl.multiple_of(j * tk, tk)
        s = jnp.dot(ka_ref[pl.ds(start, tk), :], qt, preferred_element_type=F32)
        return update(s, start, tk, carry)

    per = tk // tq
    nfull = i // per
    rem = i - nfull * per
    init = (jnp.full((1, tq), NEG, F32), jnp.zeros((1, tq), F32))
    carry = lax.fori_loop(0, nfull, full_tile, init)
    start = pl.multiple_of(nfull * tk, tk)
    for r in range(per):
        @pl.when(rem == r)
        def _():
            size = (r + 1) * tq
            s = jnp.dot(ka_ref[pl.ds(start, size), :], qt, preferred_element_type=F32)
            krow = lax.broadcasted_iota(jnp.int32, (size, tq), 0)
            qcol = lax.broadcasted_iota(jnp.int32, (size, tq), 1) + r * tq
            _, l = update(jnp.where(krow <= qcol, s, NEG), start, size, carry)
            o_ref[...] = ((acc_ref[...] / l).T * _silu(g_ref[...])).astype(BF16)


def _fox_attention(qt, ka, vt, proj, batch, seq):
    t = ka.shape[0]
    tq = min(FOX_TQ, seq)
    tk = min(FOX_TK, seq)
    nq = seq // tq
    gblk = OFF_FG // LANES
    return pl.pallas_call(
        functools.partial(_fox_kernel, tq=tq, tk=tk),
        grid=(batch, FOX_HEADS, nq),
        in_specs=[pl.BlockSpec((FOX_AUG, tq), lambda b, h, i: (h, b * nq + i)),
                  pl.BlockSpec((seq, FOX_AUG), lambda b, h, i: (b, h)),
                  pl.BlockSpec((HEAD_DIM, seq), lambda b, h, i: (h, b)),
                  pl.BlockSpec((tq, LANES), lambda b, h, i: (b * nq + i, gblk + h))],
        out_specs=pl.BlockSpec((tq, LANES), lambda b, h, i: (b * nq + i, h)),
        out_shape=jax.ShapeDtypeStruct((t, FOX_WIDTH), BF16),
        scratch_shapes=[pltpu.VMEM((HEAD_DIM, tq), F32)],
        compiler_params=_cparams("parallel", "parallel", "arbitrary"),
        name="fox_attention",
    )(qt, ka, vt, proj)


def _dsa_prep_kernel(dq_ref, iq_ref, kv_ref, misc_ref, c128, sa128, sb128, c64, sa64, sb64,
                     cik, saik, sbik, qw_ref, kw_ref, dqt_ref, iqt_ref, dk_ref, dvt_ref, ki_ref):
    t128 = (c128[...], sa128[...], sb128[...])
    t64 = (c64[...], sa64[...], sb64[...])
    qw = qw_ref[...] * (ATT_SCALE * LOG2E)
    kw = kw_ref[...]
    for h in range(DSA_HEADS):
        hs = slice(h * HEAD_DIM, (h + 1) * HEAD_DIM)
        dqt_ref[hs, :] = _rope(_rms(dq_ref[:, hs], qw), *t128, ROPE_HALF).T.astype(BF16)
    for p in range(IDX_WIDTH // LANES):
        ps = slice(p * LANES, (p + 1) * LANES)
        iqt_ref[ps, :] = _rope(iq_ref[:, ps], *t64, IDX_ROPE_HALF).T.astype(BF16)
    for g in range(DSA_KV_HEADS):
        gs = slice(g * HEAD_DIM, (g + 1) * HEAD_DIM)
        dk_ref[:, gs] = _rope(_rms(kv_ref[:, gs], kw), *t128, ROPE_HALF).astype(BF16)
        dvt_ref[gs, :] = kv_ref[:, DSA_KV_WIDTH + g * HEAD_DIM:
                                DSA_KV_WIDTH + (g + 1) * HEAD_DIM].T.astype(BF16)
    ik = _rope(misc_ref[...], cik[...], saik[...], sbik[...], IDX_ROPE_HALF)
    ki_ref[...] = (ik + pltpu.roll(ik, LANES // 2, 1)).astype(BF16)


def _dsa_prep(proj, tabs, qw, kw, seq, off_dk, misc_blk):
    t = proj.shape[0]
    tp = min(PREP_TP, seq)
    tab = pl.BlockSpec((tp, LANES), lambda i: (i, 0))
    row = pl.BlockSpec((1, LANES), lambda i: (0, 0))
    return pl.pallas_call(
        _dsa_prep_kernel,
        grid=(t // tp,),
        in_specs=[pl.BlockSpec((tp, DSA_WIDTH), lambda i: (i, OFF_DQ // DSA_WIDTH)),
                  pl.BlockSpec((tp, IDX_WIDTH), lambda i: (i, OFF_IQ // IDX_WIDTH)),
                  pl.BlockSpec((tp, 2 * DSA_KV_WIDTH), lambda i: (i, off_dk // (2 * DSA_KV_WIDTH))),
                  pl.BlockSpec((tp, LANES), lambda i: (i, misc_blk))] + [tab] * 9 + [row, row],
        out_specs=[pl.BlockSpec((DSA_WIDTH, tp), lambda i: (0, i)),
                   pl.BlockSpec((IDX_WIDTH, tp), lambda i: (0, i)),
                   pl.BlockSpec((tp, DSA_KV_WIDTH), lambda i: (i, 0)),
                   pl.BlockSpec((DSA_KV_WIDTH, tp), lambda i: (0, i)),
                   pl.BlockSpec((tp, LANES), lambda i: (i, 0))],
        out_shape=[jax.ShapeDtypeStruct((DSA_WIDTH, t), BF16),
                   jax.ShapeDtypeStruct((IDX_WIDTH, t), BF16),
                   jax.ShapeDtypeStruct((t, DSA_KV_WIDTH), BF16),
                   jax.ShapeDtypeStruct((DSA_KV_WIDTH, t), BF16),
                   jax.ShapeDtypeStruct((t, LANES), BF16)],
        compiler_params=_cparams("parallel"),
        name="dsa_prep",
    )(proj, proj, proj, proj, *tabs, qw.reshape(1, LANES), kw.reshape(1, LANES))


def _dsa_kernel(dqt_ref, dk_ref, dvt_ref, iqt_ref, ki_ref, misc_ref, g_ref, o_ref,
                keys_ref, qm_ref, acc_ref, *, tq, tk, topk):
    i = pl.program_id(1)
    q0 = i * tq
    nkt = (q0 + tq - 1) // tk + 1
    krow = lax.broadcasted_iota(jnp.int32, (tk, tq), 0)
    qcol = q0 + lax.broadcasted_iota(jnp.int32, (tk, tq), 1)
    w_t = (misc_ref[...] * IDX_SCALE).T

    zero = jnp.zeros((LANES - IDX_DIM, tq), BF16)
    for hh in range(IDX_HEADS):
        qm_ref[hh] = jnp.concatenate([iqt_ref[hh * IDX_DIM:(hh + 1) * IDX_DIM, :], zero], axis=0)

    def score_tile(j, carry):
        start = pl.multiple_of(j * tk, tk)
        kk = ki_ref[pl.ds(start, tk), :]
        acc = jnp.zeros((tk, tq), F32)
        for hh in range(IDX_HEADS):
            s = jnp.dot(kk, qm_ref[hh], preferred_element_type=F32)
            acc = acc + w_t[MISC_IW + hh:MISC_IW + hh + 1, :] * jnp.maximum(s, 0.0)
        bits = pltpu.bitcast(acc, jnp.int32)
        key = bits ^ ((bits >> 31) & 0x7FFFFFFF)
        keys_ref[pl.ds(start, tk), :] = jnp.where(start + krow <= qcol, key, INT_MIN)
        return carry

    lax.fori_loop(0, nkt, score_tile, 0)

    def count_ge(cand):
        def body(j, c):
            start = pl.multiple_of(j * tk, tk)
            ge = jnp.where(keys_ref[pl.ds(start, tk), :] >= cand, 1.0, 0.0)
            return c + jnp.sum(ge.reshape(tk // CNT_ROWS, CNT_ROWS, tq), axis=0)
        c = lax.fori_loop(0, nkt, body, jnp.zeros((CNT_ROWS, tq), F32))
        return jnp.sum(c, axis=0, keepdims=True)

    def search(p, carry):
        thr, cnt = carry
        cand = thr + jnp.left_shift(jnp.int32(1), 31 - p)
        c = count_ge(cand)
        return jnp.where(c >= topk, cand, thr), jnp.where(c >= topk, c, cnt)

    thr, cnt_ge = lax.fori_loop(
        0, 32, search, (jnp.full((1, tq), INT_MIN, jnp.int32),
                        jnp.zeros((1, tq), F32) + (nkt * tk).astype(F32)))

    tied = jnp.where((cnt_ge > topk) & (thr > INT_MIN), 1.0, 0.0)

    @pl.when(jnp.max(tied) > 0.0)
    def _():
        quota = topk - count_ge(thr + 1)
        ur = lax.broadcasted_iota(jnp.int32, (tk, tk), 0)
        uc = lax.broadcasted_iota(jnp.int32, (tk, tk), 1)
        tril = jnp.where(uc <= ur, 1.0, 0.0).astype(BF16)

        def fix(j, seen):
            start = pl.multiple_of(j * tk, tk)
            k = keys_ref[pl.ds(start, tk), :]
            eq = jnp.where(k == thr, 1.0, 0.0)
            rank = seen + jnp.dot(tril, eq.astype(BF16), preferred_element_type=F32)
            drop = (eq * tied) * jnp.where(rank > quota, 1.0, 0.0)
            keys_ref[pl.ds(start, tk), :] = jnp.where(drop > 0.0, thr - 1, k)
            return seen + jnp.sum(eq, axis=0, keepdims=True)

        lax.fori_loop(0, nkt, fix, jnp.zeros((1, tq), F32))

    thr = jnp.maximum(thr, INT_MIN + 1)

    rep = DSA_HEADS // DSA_KV_HEADS
    qs = [jnp.concatenate([dqt_ref[(g * rep + r) * HEAD_DIM:(g * rep + r + 1) * HEAD_DIM, :]
                           for r in range(rep)], axis=1) for g in range(DSA_KV_HEADS)]
    acc_ref[...] = jnp.zeros_like(acc_ref)

    def attend(j, carry):
        start = pl.multiple_of(j * tk, tk)
        nb = jnp.where(keys_ref[pl.ds(start, tk), :] >= thr, 0.0, NEG)
        nb = jnp.concatenate([nb] * rep, axis=1)
        out = []
        for g in range(DSA_KV_HEADS):
            m, l = carry[2 * g:2 * g + 2]
            gs = slice(g * HEAD_DIM, (g + 1) * HEAD_DIM)
            s = jnp.dot(dk_ref[pl.ds(start, tk), gs], qs[g], preferred_element_type=F32) + nb
            m_new = jnp.maximum(m, jnp.max(s, axis=0, keepdims=True))
            alpha = jnp.exp2(m - m_new)
            p = jnp.exp2(s - m_new)
            l = alpha * l + jnp.sum(p, axis=0, keepdims=True)
            acc_ref[g] = alpha * acc_ref[g] + jnp.dot(
                dvt_ref[gs, pl.ds(start, tk)], p.astype(BF16), preferred_element_type=F32)
            out.extend([m_new, l])
        return tuple(out)

    init = (jnp.full((1, rep * tq), M_INIT, F32), jnp.zeros((1, rep * tq), F32)) * DSA_KV_HEADS
    res = lax.fori_loop(0, nkt, attend, init)
    for g in range(DSA_KV_HEADS):
        o = acc_ref[g] / res[2 * g + 1]
        for r in range(rep):
            sl = slice((g * rep + r) * HEAD_DIM, (g * rep + r + 1) * HEAD_DIM)
            o_ref[:, sl] = (o[:, r * tq:(r + 1) * tq].T * _silu(g_ref[:, sl])).astype(BF16)


def _dsa_attention(dqt, dk, dvt, iqt, ki, proj, batch, seq, misc_blk):
    t = dk.shape[0]
    tq = min(DSA_TQ, seq)
    tk = min(ATT_TK, seq)
    nq = seq // tq
    topk = min(TOPK_MAX, seq // 4)
    rep = DSA_HEADS // DSA_KV_HEADS
    return pl.pallas_call(
        functools.partial(_dsa_kernel, tq=tq, tk=tk, topk=topk),
        grid=(batch, nq),
        in_specs=[pl.BlockSpec((DSA_WIDTH, tq), lambda b, i: (0, b * nq + i)),
                  pl.BlockSpec((seq, DSA_KV_WIDTH), lambda b, i: (b, 0)),
                  pl.BlockSpec((DSA_KV_WIDTH, seq), lambda b, i: (0, b)),
                  pl.BlockSpec((IDX_WIDTH, tq), lambda b, i: (0, b * nq + i)),
                  pl.BlockSpec((seq, LANES), lambda b, i: (b, 0)),
                  pl.BlockSpec((tq, LANES), lambda b, i: (b * nq + i, misc_blk)),
                  pl.BlockSpec((tq, DSA_WIDTH), lambda b, i: (b * nq + i, OFF_DG // DSA_WIDTH))],
        out_specs=pl.BlockSpec((tq, DSA_WIDTH), lambda b, i: (b * nq + i, 0)),
        out_shape=jax.ShapeDtypeStruct((t, DSA_WIDTH), BF16),
        scratch_shapes=[pltpu.VMEM((seq, tq), jnp.int32),
                        pltpu.VMEM((IDX_HEADS, LANES, tq), BF16),
                        pltpu.VMEM((DSA_KV_HEADS, HEAD_DIM, rep * tq), F32)],
        compiler_params=_cparams("parallel", "arbitrary"),
        name="dsa_attention",
    )(dqt, dk, dvt, iqt, ki, proj, proj)


def _ssd_kernel(xin_ref, z_ref, misc_ref, cw_ref, cb_ref, dtb_ref, alog_ref, dsk_ref, nw_ref,
                e_ref, o_ref, xbuf, xbc, ybuf, state, *, L):
    @pl.when(pl.program_id(1) == 0)
    def _():
        xbuf[0:8, :] = jnp.zeros((8, CONV_CH), F32)
        state[...] = jnp.zeros_like(state)

    xbuf[8:8 + L, :] = xin_ref[...]
    lc = 512
    for ch in range(CONV_CH // lc):
        ls = slice(ch * lc, (ch + 1) * lc)
        acc = cb_ref[:, ls] + cw_ref[3:4, ls] * xbuf[8:8 + L, ls]
        acc = acc + cw_ref[2:3, ls] * xbuf[7:7 + L, ls]
        acc = acc + cw_ref[1:2, ls] * xbuf[6:6 + L, ls]
        acc = acc + cw_ref[0:1, ls] * xbuf[5:5 + L, ls]
        xbc[:, ls] = _silu(acc)
    xbuf[0:8, :] = xbuf[L:L + 8, :]

    dt = _softplus(misc_ref[...] + dtb_ref[...])
    a = -jnp.exp(alog_ref[...])
    acs = _cumsum_rows(dt * a)
    ea = jnp.exp(acs)
    dte = jnp.exp(acs[L - 1:L, :] - acs)
    full = _dot_exact_rhs(jnp.concatenate([dt, ea, dte], axis=0), e_ref[...])
    dt_f = full[0:L]
    ea_f = full[L:2 * L]
    dte_f = full[2 * L:3 * L]
    acs_t = acs.T
    lane = lax.broadcasted_iota(jnp.int32, (L, LANES), 1)
    r = lax.broadcasted_iota(jnp.int32, (L, L), 0)
    c = lax.broadcasted_iota(jnp.int32, (L, L), 1)
    causal = c <= r

    gw = SSD_WIDTH // SSD_GROUPS
    hpg = SSD_HEADS // SSD_GROUPS
    for g in range(SSD_GROUPS):
        gs = slice(g * gw, (g + 1) * gw)
        xd = xbc[:, gs] * dt_f[:, gs]
        xd_b = xd.astype(BF16)
        bg = xbc[:, SSD_WIDTH + g * SSD_STATE:SSD_WIDTH + (g + 1) * SSD_STATE]
        cg = xbc[:, SSD_WIDTH + (SSD_GROUPS + g) * SSD_STATE:
                 SSD_WIDTH + (SSD_GROUPS + g + 1) * SSD_STATE].astype(BF16)
        cbm = lax.dot_general(cg, bg.astype(BF16), (((1,), (1,)), ((), ())),
                              preferred_element_type=F32)
        for pr in range(hpg // 2):
            xpair = xd_b[:, pr * LANES:(pr + 1) * LANES]
            ys = []
            for hh in range(2):
                hl = MISC_DT + g * hpg + 2 * pr + hh
                seg = jnp.where(causal, acs[:, hl:hl + 1] - acs_t[hl:hl + 1, :], NEG)
                mm = (cbm * jnp.exp(seg)).astype(BF16)
                ys.append(jnp.dot(mm, xpair, preferred_element_type=F32))
            ybuf[:, g * gw + pr * LANES:g * gw + (pr + 1) * LANES] = jnp.where(
                lane < SSD_HEAD_DIM, ys[0], ys[1])
        st = state[:, gs]
        yoff = jnp.dot(cg, st.astype(BF16), preferred_element_type=F32)
        ybuf[:, gs] = ybuf[:, gs] + yoff * ea_f[:, gs]
        zz = (xd * dte_f[:, gs]).astype(BF16)
        state[:, gs] = st * ea_f[L - 1:L, gs] + jnp.dot(bg.T.astype(BF16), zz,
                                                        preferred_element_type=F32)
        y = ybuf[:, gs] + dsk_ref[:, gs] * xbc[:, gs]
        y = y * _silu(z_ref[:, gs])
        o_ref[:, gs] = _rms(y, nw_ref[:, gs]).astype(BF16)


def _ssd(proj, conv_w, conv_b, dtb128, alog128, dskip_f, ssd_norm, expand, batch, seq, misc_blk):
    t = proj.shape[0]
    L = SSD_CHUNK
    nc = seq // L
    const = lambda shape: pl.BlockSpec(shape, lambda b, c: (0, 0))
    return pl.pallas_call(
        functools.partial(_ssd_kernel, L=L),
        grid=(batch, nc),
        in_specs=[pl.BlockSpec((L, CONV_CH), lambda b, c: (b * nc + c, 0)),
                  pl.BlockSpec((L, SSD_WIDTH), lambda b, c: (b * nc + c, OFF_SZ // SSD_WIDTH)),
                  pl.BlockSpec((L, LANES), lambda b, c: (b * nc + c, misc_blk)),
                  const((CONV_WIDTH, CONV_CH)), const((1, CONV_CH)), const((1, LANES)),
                  const((1, LANES)), const((1, SSD_WIDTH)), const((1, SSD_WIDTH)),
                  const((LANES, SSD_WIDTH))],
        out_specs=pl.BlockSpec((L, SSD_WIDTH), lambda b, c: (b * nc + c, 0)),
        out_shape=jax.ShapeDtypeStruct((t, SSD_WIDTH), BF16),
        scratch_shapes=[pltpu.VMEM((L + 8, CONV_CH), F32), pltpu.VMEM((L, CONV_CH), F32),
                        pltpu.VMEM((L, SSD_WIDTH), F32), pltpu.VMEM((SSD_STATE, SSD_WIDTH), F32)],
        compiler_params=_cparams("parallel", "arbitrary"),
        name="ssd_scan",
    )(proj, proj, proj, conv_w, conv_b.reshape(1, CONV_CH), dtb128, alog128, dskip_f,
      ssd_norm.reshape(1, SSD_WIDTH), expand)


def _merge_kernel(yf_ref, yd_ref, ys_ref, wf_ref, wd_ref, ws_ref, g0_ref, g1_ref, g2_ref,
                  bm_ref, o_ref):
    a = jnp.dot(yf_ref[...], wf_ref[...], preferred_element_type=F32)
    b = jnp.dot(yd_ref[...], wd_ref[...], preferred_element_type=F32)
    c = jnp.dot(ys_ref[...], ws_ref[...], preferred_element_type=F32)
    m = (jax.nn.sigmoid(g0_ref[...] + bm_ref[0:1, :]) * a
         + jax.nn.sigmoid(g1_ref[...] + bm_ref[1:2, :]) * b
         + jax.nn.sigmoid(g2_ref[...] + bm_ref[2:3, :]) * c)
    o_ref[...] = m.astype(BF16)


def _merge(y_fox, y_dsa, y_ssd, wf, wd, ws, proj, b_merge):
    t = y_fox.shape[0]
    d = wf.shape[1]
    tm = OUT_TM
    tn = OUT_TN
    gb = OFF_MG // tn
    gspec = lambda br: pl.BlockSpec((tm, tn), lambda j, i: (i, gb + br * (d // tn) + j))
    return pl.pallas_call(
        _merge_kernel,
        grid=(d // tn, t // tm),
        in_specs=[pl.BlockSpec((tm, FOX_WIDTH), lambda j, i: (i, 0)),
                  pl.BlockSpec((tm, DSA_WIDTH), lambda j, i: (i, 0)),
                  pl.BlockSpec((tm, SSD_WIDTH), lambda j, i: (i, 0)),
                  pl.BlockSpec((FOX_WIDTH, tn), lambda j, i: (0, j)),
                  pl.BlockSpec((DSA_WIDTH, tn), lambda j, i: (0, j)),
                  pl.BlockSpec((SSD_WIDTH, tn), lambda j, i: (0, j)),
                  gspec(0), gspec(1), gspec(2),
                  pl.BlockSpec((N_BRANCH, tn), lambda j, i: (0, j))],
        out_specs=pl.BlockSpec((tm, tn), lambda j, i: (i, j)),
        out_shape=jax.ShapeDtypeStruct((t, d), BF16),
        compiler_params=_cparams("parallel", "arbitrary"),
        name="branch_merge",
    )(y_fox, y_dsa, y_ssd, wf, wd, ws, proj, proj, proj, b_merge)


def _outproj_kernel(m_ref, w_ref, x_ref, g_ref, o_ref):
    out = jnp.dot(m_ref[...], w_ref[...], preferred_element_type=F32)
    o_ref[...] = x_ref[...] + g_ref[...] * out


def _outproj(merged, w_out, x2, gate, seq):
    t, d = x2.shape
    tm = OUT_TM
    tn = OUT_TN
    return pl.pallas_call(
        _outproj_kernel,
        grid=(d // tn, t // tm),
        in_specs=[pl.BlockSpec((tm, d), lambda j, i: (i, 0)),
                  pl.BlockSpec((d, tn), lambda j, i: (0, j)),
                  pl.BlockSpec((tm, tn), lambda j, i: (i, j)),
                  pl.BlockSpec((None, 1, tn), lambda j, i: (i * tm // seq, 0, j))],
        out_specs=pl.BlockSpec((tm, tn), lambda j, i: (i, j)),
        out_shape=jax.ShapeDtypeStruct((t, d), F32),
        compiler_params=_cparams("parallel", "arbitrary"),
        name="out_proj_residual",
    )(merged, w_out, x2, gate)


def _lane_block(vec, off):
    return jnp.zeros((1, LANES), F32).at[0, off:off + vec.shape[0]].set(vec.astype(F32))


def kernel(x, c, positions, norm_w, w_ada, b_ada, w_in, b_fox_f, fox_q_norm, fox_k_norm, dsa_q_norm, dsa_k_norm, conv_w, conv_b, dt_bias, a_log, d_skip, ssd_norm, b_merge, w_o_fox, w_o_dsa, w_o_ssd, w_out):
    batch, seq, d = x.shape
    depth = w_in.shape[0]
    t = batch * seq
    assert seq % 512 == 0 and d % 512 == 0

    c8 = jnp.zeros((8, d), F32).at[:batch].set(c)
    mod = _ada(c8, w_ada, b_ada)
    tabs = _rope_tables(positions.reshape(t, 1))

    expand = np.zeros((LANES, SSD_WIDTH), np.float32)
    for hh in range(SSD_HEADS):
        expand[MISC_DT + hh, hh * SSD_HEAD_DIM:(hh + 1) * SSD_HEAD_DIM] = 1.0
    expand = jnp.asarray(expand, BF16)

    off_dk = OFF_MG + N_BRANCH * d
    misc_blk = (off_dk + 2 * DSA_KV_WIDTH) // LANES

    h = x.reshape(t, d)
    for l in range(depth):
        shift = mod[l, :batch, 0:d].reshape(batch, 1, d)
        scale = mod[l, :batch, d:2 * d].reshape(batch, 1, d)
        gate = mod[l, :batch, 2 * d:3 * d].reshape(batch, 1, d)
        proj = _inproj(h, norm_w[l], scale, shift, _pack_w_in(w_in[l]), seq)

        fqt, fka, fvt = _fox_prep(proj, _lane_block(b_fox_f[l], MISC_FF), fox_q_norm[l],
                                  fox_k_norm[l], seq, misc_blk)
        y_fox = _fox_attention(fqt, fka, fvt, proj, batch, seq)

        dqt, iqt, dk, dvt, ki = _dsa_prep(proj, tabs, dsa_q_norm[l], dsa_k_norm[l], seq,
                                          off_dk, misc_blk)
        y_dsa = _dsa_attention(dqt, dk, dvt, iqt, ki, proj, batch, seq, misc_blk)

        y_ssd = _ssd(proj, conv_w[l], conv_b[l], _lane_block(dt_bias[l], MISC_DT),
                     _lane_block(a_log[l], MISC_DT),
                     jnp.repeat(d_skip[l].astype(F32), SSD_HEAD_DIM).reshape(1, SSD_WIDTH),
                     ssd_norm[l], expand, batch, seq, misc_blk)

        merged = _merge(y_fox, y_dsa, y_ssd, w_o_fox[l].astype(BF16), w_o_dsa[l].astype(BF16),
                        w_o_ssd[l].astype(BF16), proj, b_merge[l])
        h = _outproj(merged, w_out[l].astype(BF16), h, gate, seq)
    return h.reshape(batch, seq, d)
```

```python
import functools
import math

import numpy as np
import jax
import jax.numpy as jnp
from jax import lax
from jax.experimental import pallas as pl
from jax.experimental.pallas import tpu as pltpu

F32 = jnp.float32
BF16 = jnp.bfloat16
HIGHEST = lax.Precision.HIGHEST

HEAD_DIM = 128
FOX_HEADS = 8
FOX_WIDTH = FOX_HEADS * HEAD_DIM
DSA_HEADS = 8
DSA_KV_HEADS = 2
DSA_WIDTH = DSA_HEADS * HEAD_DIM
DSA_KV_WIDTH = DSA_KV_HEADS * HEAD_DIM
IDX_HEADS = 16
IDX_DIM = 64
IDX_WIDTH = IDX_HEADS * IDX_DIM
TOPK_MAX = 256
SSD_HEADS = 32
SSD_HEAD_DIM = 64
SSD_WIDTH = SSD_HEADS * SSD_HEAD_DIM
SSD_GROUPS = 4
SSD_STATE = 128
SSD_CHUNK = 128
CONV_WIDTH = 4
CONV_CH = SSD_WIDTH + 2 * SSD_GROUPS * SSD_STATE
N_BRANCH = 3
ROPE_THETA = 500000.0
ROPE_FRACTION = 4
EPS = 1e-6

LANES = 128
NEG = -1e30
M_INIT = -1e29
INT_MIN = -(2 ** 31)
VMEM_LIMIT = 52 * 1024 * 1024
LOG2E = math.log2(math.e)
ATT_SCALE = HEAD_DIM ** -0.5
IDX_SCALE = (IDX_DIM ** -0.5) * (IDX_HEADS ** -0.5)
ROPE_HALF = HEAD_DIM // ROPE_FRACTION // 2
IDX_ROPE_HALF = IDX_DIM // ROPE_FRACTION // 2
FOX_AUG = 2 * HEAD_DIM

OFF_XBC = 0
OFF_FQ = OFF_XBC + CONV_CH
OFF_FK = OFF_FQ + FOX_WIDTH
OFF_FV = OFF_FK + FOX_WIDTH
OFF_FG = OFF_FV + FOX_WIDTH
OFF_DQ = OFF_FG + FOX_WIDTH
OFF_IQ = OFF_DQ + DSA_WIDTH
OFF_DG = OFF_IQ + IDX_WIDTH
OFF_SZ = OFF_DG + DSA_WIDTH
OFF_MG = OFF_SZ + SSD_WIDTH
MISC_IK = 0
MISC_FF = MISC_IK + IDX_DIM
MISC_IW = MISC_FF + FOX_HEADS
MISC_DT = MISC_IW + IDX_HEADS
IN_TN = 1280
OUT_TM = 512
OUT_TN = 1024
PREP_TP = 512
ATT_TK = 512
FOX_TQ = 512
FOX_TK = 4096
DSA_TQ = 256
CNT_ROWS = 64


def _cparams(*sem):
    return pltpu.CompilerParams(dimension_semantics=sem, vmem_limit_bytes=VMEM_LIMIT)


def _silu(x):
    return x * jax.nn.sigmoid(x)


def _softplus(x):
    return jnp.maximum(x, 0.0) + jnp.log1p(jnp.exp(-jnp.abs(x)))


def _log_sigmoid(x):
    return jnp.minimum(x, 0.0) - jnp.log1p(jnp.exp(-jnp.abs(x)))


def _rms(x, w):
    return x * lax.rsqrt(jnp.mean(x * x, axis=-1, keepdims=True) + EPS) * w


def _split3(x):
    x1 = x.astype(BF16)
    r = x - x1.astype(F32)
    x2 = r.astype(BF16)
    x3 = (r - x2.astype(F32)).astype(BF16)
    return x1, x2, x3


def _dot_exact_rhs(x, m_bf16):
    x1, x2, x3 = _split3(x)
    return (jnp.dot(x1, m_bf16, preferred_element_type=F32)
            + jnp.dot(x2, m_bf16, preferred_element_type=F32)
            + jnp.dot(x3, m_bf16, preferred_element_type=F32))


def _cumsum_rows(x):
    n = x.shape[0]
    r = lax.broadcasted_iota(jnp.int32, (n, n), 0)
    c = lax.broadcasted_iota(jnp.int32, (n, n), 1)
    tril = jnp.where(c <= r, 1.0, 0.0).astype(BF16)
    x1, x2, x3 = _split3(x)
    return (jnp.dot(tril, x1, preferred_element_type=F32)
            + jnp.dot(tril, x2, preferred_element_type=F32)
            + jnp.dot(tril, x3, preferred_element_type=F32))


def _rope(x, c, sa, sb, half):
    return x * c + pltpu.roll(x, LANES - half, 1) * sa + pltpu.roll(x, half, 1) * sb


def _ada_kernel(c_ref, w_ref, b_ref, o_ref):
    s = _silu(c_ref[...])
    o_ref[0] = jnp.dot(s, w_ref[0], precision=HIGHEST, preferred_element_type=F32) + b_ref[0]


def _ada(c8, w_ada, b_ada):
    depth, d, n = w_ada.shape
    tn = 512
    return pl.pallas_call(
        _ada_kernel,
        grid=(depth, n // tn),
        in_specs=[pl.BlockSpec((8, d), lambda l, j: (0, 0)),
                  pl.BlockSpec((1, d, tn), lambda l, j: (l, 0, j)),
                  pl.BlockSpec((1, 1, tn), lambda l, j: (l, 0, j))],
        out_specs=pl.BlockSpec((1, 8, tn), lambda l, j: (l, 0, j)),
        out_shape=jax.ShapeDtypeStruct((depth, 8, n), F32),
        compiler_params=_cparams("parallel", "parallel"),
        name="ada_mod",
    )(c8, w_ada, b_ada.reshape(depth, 1, n))


def _inproj_kernel(x_ref, nw_ref, sc_ref, sh_ref, w_ref, o_ref, u_ref, *, rows):
    @pl.when(pl.program_id(1) == 0)
    def _():
        nw = nw_ref[...]
        sc = 1.0 + sc_ref[...]
        sh = sh_ref[...]

        def body(r, carry):
            sl = pl.ds(pl.multiple_of(r * rows, rows), rows)
            u_ref[sl, :] = (_rms(x_ref[sl, :], nw) * sc + sh).astype(BF16)
            return carry

        lax.fori_loop(0, x_ref.shape[0] // rows, body, 0)

    o_ref[...] = jnp.dot(u_ref[...], w_ref[...], preferred_element_type=F32)


def _inproj(x2, norm_w, scale, shift, w_packed, seq):
    t, d = x2.shape
    n = w_packed.shape[1]
    tm = min(1024, seq)
    return pl.pallas_call(
        functools.partial(_inproj_kernel, rows=64),
        grid=(t // tm, n // IN_TN),
        in_specs=[pl.BlockSpec((tm, d), lambda i, j: (i, 0)),
                  pl.BlockSpec((1, d), lambda i, j: (0, 0)),
                  pl.BlockSpec((None, 1, d), lambda i, j: (i * tm // seq, 0, 0)),
                  pl.BlockSpec((None, 1, d), lambda i, j: (i * tm // seq, 0, 0)),
                  pl.BlockSpec((d, IN_TN), lambda i, j: (0, j))],
        out_specs=pl.BlockSpec((tm, IN_TN), lambda i, j: (i, j)),
        out_shape=jax.ShapeDtypeStruct((t, n), F32),
        scratch_shapes=[pltpu.VMEM((tm, d), BF16)],
        compiler_params=_cparams("parallel", "arbitrary"),
        name="norm_inproj",
    )(x2, norm_w.reshape(1, d), scale, shift, w_packed)


def _pack_w_in(w):
    sizes = (FOX_WIDTH, FOX_WIDTH, FOX_WIDTH, FOX_HEADS, FOX_WIDTH,
             DSA_WIDTH, DSA_KV_WIDTH, DSA_KV_WIDTH,
             IDX_WIDTH, IDX_DIM, IDX_HEADS, DSA_WIDTH,
             SSD_WIDTH, CONV_CH, SSD_HEADS, N_BRANCH * w.shape[0])
    offs = np.concatenate([[0], np.cumsum(sizes)])
    (fq, fk, fv, ff, fg, dq, dk, dv, iq, ik, iw, dg, sz, sxbc, sdt, mg) = [
        w[:, int(offs[i]):int(offs[i + 1])].astype(BF16) for i in range(len(sizes))]
    d = w.shape[0]
    pad8 = jnp.zeros((d, LANES - (IDX_DIM + FOX_HEADS + IDX_HEADS + SSD_HEADS)), BF16)
    cols = [sxbc, fq, fk, fv, fg, dq, iq, dg, sz, mg, dk, dv, ik, ff, iw, sdt, pad8]
    packed = jnp.concatenate(cols, axis=1)
    pad = (-packed.shape[1]) % IN_TN
    return jnp.concatenate([packed, jnp.zeros((d, pad), BF16)], axis=1)


def _rope_tab_kernel(pos_ref, c128, sa128, sb128, c64, sa64, sb64, cik, saik, sbik):
    pos = pos_ref[...].astype(F32)
    lane = lax.broadcasted_iota(jnp.int32, (8, LANES), 1)[0:1, :]

    def tables(d):
        rd = d // ROPE_FRACTION
        half = rd // 2
        j = lane % d
        inv = jnp.power(ROPE_THETA, -(j % half).astype(F32) / half)
        ang = pos * inv
        cos = jnp.cos(ang)
        sin = jnp.sin(ang)
        c = jnp.where(j < rd, cos, 1.0)
        sa = jnp.where(j < half, -sin, 0.0)
        sb = jnp.where(j < half, 0.0, jnp.where(j < rd, sin, 0.0))
        return c, sa, sb

    c, sa, sb = tables(HEAD_DIM)
    c128[...] = c
    sa128[...] = sa
    sb128[...] = sb
    c, sa, sb = tables(IDX_DIM)
    c64[...] = c
    sa64[...] = sa
    sb64[...] = sb
    keep = lane < IDX_DIM
    cik[...] = jnp.where(keep, c, 0.0)
    saik[...] = jnp.where(keep, sa, 0.0)
    sbik[...] = jnp.where(keep, sb, 0.0)


def _rope_tables(pos2):
    t = pos2.shape[0]
    tp = min(PREP_TP, t)
    spec = pl.BlockSpec((tp, LANES), lambda i: (i, 0))
    return pl.pallas_call(
        _rope_tab_kernel,
        grid=(t // tp,),
        in_specs=[pl.BlockSpec((tp, 1), lambda i: (i, 0))],
        out_specs=[spec] * 9,
        out_shape=[jax.ShapeDtypeStruct((t, LANES), F32)] * 9,
        compiler_params=_cparams("parallel"),
        name="rope_tables",
    )(pos2)


def _fox_prep_kernel(q_ref, k_ref, v_ref, misc_ref, bias_ref, qw_ref, kw_ref,
                     qt_ref, ka_ref, vt_ref, carry_ref, *, tiles_per_batch):
    @pl.when(pl.program_id(0) % tiles_per_batch == 0)
    def _():
        carry_ref[...] = jnp.zeros_like(carry_ref)

    tp = q_ref.shape[0]
    cs = _cumsum_rows(_log_sigmoid(misc_ref[...] + bias_ref[...])) + carry_ref[0:1, :]
    carry_ref[...] = jnp.broadcast_to(cs[tp - 1:tp, :], carry_ref.shape)
    f = cs * LOG2E
    f1 = f.astype(BF16).astype(F32)
    r1 = f - f1
    f2 = r1.astype(BF16).astype(F32)
    f3 = r1 - f2
    cols = (f1, f2, f3)
    rows = tuple(x.T for x in cols)
    sub = lax.broadcasted_iota(jnp.int32, (8, tp), 0)
    lane = lax.broadcasted_iota(jnp.int32, (tp, LANES), 1)
    zeros_t = jnp.zeros((HEAD_DIM - 8, tp), F32)
    qw = qw_ref[...] * (ATT_SCALE * LOG2E)
    kw = kw_ref[...]
    for h in range(FOX_HEADS):
        hs = slice(h * HEAD_DIM, (h + 1) * HEAD_DIM)
        hl = MISC_FF + h
        qn = _rms(q_ref[:, hs], qw)
        aug = jnp.where(sub == 0, rows[0][hl:hl + 1, :],
                        jnp.where(sub == 1, rows[1][hl:hl + 1, :],
                                  jnp.where(sub == 2, rows[2][hl:hl + 1, :],
                                            jnp.where(sub < 6, 1.0, 0.0))))
        qt_ref[h * FOX_AUG:h * FOX_AUG + HEAD_DIM, :] = qn.T.astype(BF16)
        qt_ref[h * FOX_AUG + HEAD_DIM:(h + 1) * FOX_AUG, :] = jnp.concatenate(
            [aug, zeros_t], axis=0).astype(BF16)
        ka_ref[:, h * FOX_AUG:h * FOX_AUG + HEAD_DIM] = _rms(k_ref[:, hs], kw).astype(BF16)
        kaug = jnp.where(lane < 3, 1.0,
                         jnp.where(lane == 3, -cols[0][:, hl:hl + 1],
                                   jnp.where(lane == 4, -cols[1][:, hl:hl + 1],
                                             jnp.where(lane == 5, -cols[2][:, hl:hl + 1], 0.0))))
        ka_ref[:, h * FOX_AUG + HEAD_DIM:(h + 1) * FOX_AUG] = kaug.astype(BF16)
        vt_ref[hs, :] = v_ref[:, hs].T.astype(BF16)


def _fox_prep(proj, bias128, qw, kw, seq, misc_blk):
    t = proj.shape[0]
    tp = min(PREP_TP, seq)
    wide = lambda off: pl.BlockSpec((tp, FOX_WIDTH), lambda i: (i, off // FOX_WIDTH))
    row = pl.BlockSpec((1, LANES), lambda i: (0, 0))
    return pl.pallas_call(
        functools.partial(_fox_prep_kernel, tiles_per_batch=seq // tp),
        grid=(t // tp,),
        in_specs=[wide(OFF_FQ), wide(OFF_FK), wide(OFF_FV),
                  pl.BlockSpec((tp, LANES), lambda i: (i, misc_blk)), row, row, row],
        out_specs=[pl.BlockSpec((FOX_HEADS * FOX_AUG, tp), lambda i: (0, i)),
                   pl.BlockSpec((tp, FOX_HEADS * FOX_AUG), lambda i: (i, 0)),
                   pl.BlockSpec((FOX_WIDTH, tp), lambda i: (0, i))],
        out_shape=[jax.ShapeDtypeStruct((FOX_HEADS * FOX_AUG, t), BF16),
                   jax.ShapeDtypeStruct((t, FOX_HEADS * FOX_AUG), BF16),
                   jax.ShapeDtypeStruct((FOX_WIDTH, t), BF16)],
        scratch_shapes=[pltpu.VMEM((8, LANES), F32)],
        compiler_params=_cparams("arbitrary"),
        name="fox_prep",
    )(proj, proj, proj, proj, bias128, qw.reshape(1, LANES), kw.reshape(1, LANES))


def _fox_kernel(qt_ref, ka_ref, vt_ref, g_ref, o_ref, acc_ref, *, tq, tk):
    i = pl.program_id(2)
    qt = qt_ref[...]
    acc_ref[...] = jnp.zeros_like(acc_ref)

    def update(s, start, size, carry):
        m, l = carry
        m_new = jnp.maximum(m, jnp.max(s, axis=0, keepdims=True))
        alpha = jnp.exp2(m - m_new)
        p = jnp.exp2(s - m_new)
        l = alpha * l + jnp.sum(p, axis=0, keepdims=True)
        acc_ref[...] = alpha * acc_ref[...] + jnp.dot(
            vt_ref[:, pl.ds(start, size)], p.astype(BF16), preferred_element_type=F32)
        return m_new, l

    def full_tile(j, carry):
        start = pl.multiple_of(j * tk, tk)
        s = jnp.dot(ka_ref[pl.ds(start, tk), :], qt, preferred_element_type=F32)
        return update(s, start, tk, carry)

    per = tk // tq
    nfull = i // per
    rem = i - nfull * per
    init = (jnp.full((1, tq), NEG, F32), jnp.zeros((1, tq), F32))
    carry = lax.fori_loop(0, nfull, full_tile, init)
    start = pl.multiple_of(nfull * tk, tk)
    for r in range(per):
        @pl.when(rem == r)
        def _():
            size = (r + 1) * tq
            s = jnp.dot(ka_ref[pl.ds(start, size), :], qt, preferred_element_type=F32)
            krow = lax.broadcasted_iota(jnp.int32, (size, tq), 0)
            qcol = lax.broadcasted_iota(jnp.int32, (size, tq), 1) + r * tq
            _, l = update(jnp.where(krow <= qcol, s, NEG), start, size, carry)
            o_ref[...] = ((acc_ref[...] / l).T * _silu(g_ref[...])).astype(BF16)


def _fox_attention(qt, ka, vt, proj, batch, seq):
    t = ka.shape[0]
    tq = min(FOX_TQ, seq)
    tk = min(FOX_TK, seq)
    nq = seq // tq
    gblk = OFF_FG // LANES
    return pl.pallas_call(
        functools.partial(_fox_kernel, tq=tq, tk=tk),
        grid=(batch, FOX_HEADS, nq),
        in_specs=[pl.BlockSpec((FOX_AUG, tq), lambda b, h, i: (h, b * nq + i)),
                  pl.BlockSpec((seq, FOX_AUG), lambda b, h, i: (b, h)),
                  pl.BlockSpec((HEAD_DIM, seq), lambda b, h, i: (h, b)),
                  pl.BlockSpec((tq, LANES), lambda b, h, i: (b * nq + i, gblk + h))],
        out_specs=pl.BlockSpec((tq, LANES), lambda b, h, i: (b * nq + i, h)),
        out_shape=jax.ShapeDtypeStruct((t, FOX_WIDTH), BF16),
        scratch_shapes=[pltpu.VMEM((HEAD_DIM, tq), F32)],
        compiler_params=_cparams("parallel", "parallel", "arbitrary"),
        name="fox_attention",
    )(qt, ka, vt, proj)


def _dsa_prep_kernel(dq_ref, iq_ref, kv_ref, misc_ref, c128, sa128, sb128, c64, sa64, sb64,
                     cik, saik, sbik, qw_ref, kw_ref, dqt_ref, iqt_ref, dk_ref, dvt_ref, ki_ref):
    t128 = (c128[...], sa128[...], sb128[...])
    t64 = (c64[...], sa64[...], sb64[...])
    qw = qw_ref[...] * (ATT_SCALE * LOG2E)
    kw = kw_ref[...]
    for h in range(DSA_HEADS):
        hs = slice(h * HEAD_DIM, (h + 1) * HEAD_DIM)
        dqt_ref[hs, :] = _rope(_rms(dq_ref[:, hs], qw), *t128, ROPE_HALF).T.astype(BF16)
    for p in range(IDX_WIDTH // LANES):
        ps = slice(p * LANES, (p + 1) * LANES)
        iqt_ref[ps, :] = _rope(iq_ref[:, ps], *t64, IDX_ROPE_HALF).T.astype(BF16)
    for g in range(DSA_KV_HEADS):
        gs = slice(g * HEAD_DIM, (g + 1) * HEAD_DIM)
        dk_ref[:, gs] = _rope(_rms(kv_ref[:, gs], kw), *t128, ROPE_HALF).astype(BF16)
        dvt_ref[gs, :] = kv_ref[:, DSA_KV_WIDTH + g * HEAD_DIM:
                                DSA_KV_WIDTH + (g + 1) * HEAD_DIM].T.astype(BF16)
    ik = _rope(misc_ref[...], cik[...], saik[...], sbik[...], IDX_ROPE_HALF)
    ki_ref[...] = (ik + pltpu.roll(ik, LANES // 2, 1)).astype(BF16)


def _dsa_prep(proj, tabs, qw, kw, seq, off_dk, misc_blk):
    t = proj.shape[0]
    tp = min(PREP_TP, seq)
    tab = pl.BlockSpec((tp, LANES), lambda i: (i, 0))
    row = pl.BlockSpec((1, LANES), lambda i: (0, 0))
    return pl.pallas_call(
        _dsa_prep_kernel,
        grid=(t // tp,),
        in_specs=[pl.BlockSpec((tp, DSA_WIDTH), lambda i: (i, OFF_DQ // DSA_WIDTH)),
                  pl.BlockSpec((tp, IDX_WIDTH), lambda i: (i, OFF_IQ // IDX_WIDTH)),
                  pl.BlockSpec((tp, 2 * DSA_KV_WIDTH), lambda i: (i, off_dk // (2 * DSA_KV_WIDTH))),
                  pl.BlockSpec((tp, LANES), lambda i: (i, misc_blk))] + [tab] * 9 + [row, row],
        out_specs=[pl.BlockSpec((DSA_WIDTH, tp), lambda i: (0, i)),
                   pl.BlockSpec((IDX_WIDTH, tp), lambda i: (0, i)),
                   pl.BlockSpec((tp, DSA_KV_WIDTH), lambda i: (i, 0)),
                   pl.BlockSpec((DSA_KV_WIDTH, tp), lambda i: (0, i)),
                   pl.BlockSpec((tp, LANES), lambda i: (i, 0))],
        out_shape=[jax.ShapeDtypeStruct((DSA_WIDTH, t), BF16),
                   jax.ShapeDtypeStruct((IDX_WIDTH, t), BF16),
                   jax.ShapeDtypeStruct((t, DSA_KV_WIDTH), BF16),
                   jax.ShapeDtypeStruct((DSA_KV_WIDTH, t), BF16),
                   jax.ShapeDtypeStruct((t, LANES), BF16)],
        compiler_params=_cparams("parallel"),
        name="dsa_prep",
    )(proj, proj, proj, proj, *tabs, qw.reshape(1, LANES), kw.reshape(1, LANES))


def _dsa_kernel(dqt_ref, dk_ref, dvt_ref, iqt_ref, ki_ref, misc_ref, g_ref, o_ref,
                keys_ref, qm_ref, acc_ref, *, tq, tk, topk):
    i = pl.program_id(1)
    q0 = i * tq
    nkt = (q0 + tq - 1) // tk + 1
    krow = lax.broadcasted_iota(jnp.int32, (tk, tq), 0)
    qcol = q0 + lax.broadcasted_iota(jnp.int32, (tk, tq), 1)
    w_t = (misc_ref[...] * IDX_SCALE).T

    zero = jnp.zeros((LANES - IDX_DIM, tq), BF16)
    for hh in range(IDX_HEADS):
        qm_ref[hh] = jnp.concatenate([iqt_ref[hh * IDX_DIM:(hh + 1) * IDX_DIM, :], zero], axis=0)

    def score_tile(j, carry):
        start = pl.multiple_of(j * tk, tk)
        kk = ki_ref[pl.ds(start, tk), :]
        acc = jnp.zeros((tk, tq), F32)
        for hh in range(IDX_HEADS):
            s = jnp.dot(kk, qm_ref[hh], preferred_element_type=F32)
            acc = acc + w_t[MISC_IW + hh:MISC_IW + hh + 1, :] * jnp.maximum(s, 0.0)
        bits = pltpu.bitcast(acc, jnp.int32)
        key = bits ^ ((bits >> 31) & 0x7FFFFFFF)
        keys_ref[pl.ds(start, tk), :] = jnp.where(start + krow <= qcol, key, INT_MIN)
        return carry

    lax.fori_loop(0, nkt, score_tile, 0)

    def count_ge(cand):
        def body(j, c):
            start = pl.multiple_of(j * tk, tk)
            ge = jnp.where(keys_ref[pl.ds(start, tk), :] >= cand, 1.0, 0.0)
            return c + jnp.sum(ge.reshape(tk // CNT_ROWS, CNT_ROWS, tq), axis=0)
        c = lax.fori_loop(0, nkt, body, jnp.zeros((CNT_ROWS, tq), F32))
        return jnp.sum(c, axis=0, keepdims=True)

    def search(p, carry):
        thr, cnt = carry
        cand = thr + jnp.left_shift(jnp.int32(1), 31 - p)
        c = count_ge(cand)
        return jnp.where(c >= topk, cand, thr), jnp.where(c >= topk, c, cnt)

    thr, cnt_ge = lax.fori_loop(
        0, 32, search, (jnp.full((1, tq), INT_MIN, jnp.int32),
                        jnp.zeros((1, tq), F32) + (nkt * tk).astype(F32)))

    tied = jnp.where((cnt_ge > topk) & (thr > INT_MIN), 1.0, 0.0)

    @pl.when(jnp.max(tied) > 0.0)
    def _():
        quota = topk - count_ge(thr + 1)
        ur = lax.broadcasted_iota(jnp.int32, (tk, tk), 0)
        uc = lax.broadcasted_iota(jnp.int32, (tk, tk), 1)
        tril = jnp.where(uc <= ur, 1.0, 0.0).astype(BF16)

        def fix(j, seen):
            start = pl.multiple_of(j * tk, tk)
            k = keys_ref[pl.ds(start, tk), :]
            eq = jnp.where(k == thr, 1.0, 0.0)
            rank = seen + jnp.dot(tril, eq.astype(BF16), preferred_element_type=F32)
            drop = (eq * tied) * jnp.where(rank > quota, 1.0, 0.0)
            keys_ref[pl.ds(start, tk), :] = jnp.where(drop > 0.0, thr - 1, k)
            return seen + jnp.sum(eq, axis=0, keepdims=True)

        lax.fori_loop(0, nkt, fix, jnp.zeros((1, tq), F32))

    thr = jnp.maximum(thr, INT_MIN + 1)

    rep = DSA_HEADS // DSA_KV_HEADS
    qs = [jnp.concatenate([dqt_ref[(g * rep + r) * HEAD_DIM:(g * rep + r + 1) * HEAD_DIM, :]
                           for r in range(rep)], axis=1) for g in range(DSA_KV_HEADS)]
    acc_ref[...] = jnp.zeros_like(acc_ref)

    def attend(start, size, carry):
        nb = jnp.where(keys_ref[pl.ds(start, size), :] >= thr, 0.0, NEG)
        nb = jnp.concatenate([nb] * rep, axis=1)
        out = []
        for g in range(DSA_KV_HEADS):
            m, l = carry[2 * g:2 * g + 2]
            gs = slice(g * HEAD_DIM, (g + 1) * HEAD_DIM)
            s = jnp.dot(dk_ref[pl.ds(start, size), gs], qs[g], preferred_element_type=F32) + nb
            m_new = jnp.maximum(m, jnp.max(s, axis=0, keepdims=True))
            alpha = jnp.exp2(m - m_new)
            p = jnp.exp2(s - m_new)
            l = alpha * l + jnp.sum(p, axis=0, keepdims=True)
            acc_ref[g] = alpha * acc_ref[g] + jnp.dot(
                dvt_ref[gs, pl.ds(start, size)], p.astype(BF16), preferred_element_type=F32)
            out.extend([m_new, l])
        return tuple(out)

    init = (jnp.full((1, rep * tq), M_INIT, F32), jnp.zeros((1, rep * tq), F32)) * DSA_KV_HEADS
    res = lax.fori_loop(
        0, nkt // 2, lambda j, c: attend(pl.multiple_of(j * 2 * tk, 2 * tk), 2 * tk, c), init)
    res = lax.cond(nkt % 2 == 1,
                   lambda c: attend(pl.multiple_of((nkt - 1) * tk, tk), tk, c),
                   lambda c: c, res)
    for g in range(DSA_KV_HEADS):
        o = acc_ref[g] / res[2 * g + 1]
        for r in range(rep):
            sl = slice((g * rep + r) * HEAD_DIM, (g * rep + r + 1) * HEAD_DIM)
            o_ref[:, sl] = (o[:, r * tq:(r + 1) * tq].T * _silu(g_ref[:, sl])).astype(BF16)


def _dsa_attention(dqt, dk, dvt, iqt, ki, proj, batch, seq, misc_blk):
    t = dk.shape[0]
    tq = min(DSA_TQ, seq)
    tk = min(ATT_TK, seq)
    nq = seq // tq
    topk = min(TOPK_MAX, seq // 4)
    rep = DSA_HEADS // DSA_KV_HEADS
    once = pl.Buffered(1)
    return pl.pallas_call(
        functools.partial(_dsa_kernel, tq=tq, tk=tk, topk=topk),
        grid=(batch, nq),
        in_specs=[pl.BlockSpec((DSA_WIDTH, tq), lambda b, i: (0, b * nq + i)),
                  pl.BlockSpec((seq, DSA_KV_WIDTH), lambda b, i: (b, 0), pipeline_mode=once),
                  pl.BlockSpec((DSA_KV_WIDTH, seq), lambda b, i: (0, b), pipeline_mode=once),
                  pl.BlockSpec((IDX_WIDTH, tq), lambda b, i: (0, b * nq + i)),
                  pl.BlockSpec((seq, LANES), lambda b, i: (b, 0), pipeline_mode=once),
                  pl.BlockSpec((tq, LANES), lambda b, i: (b * nq + i, misc_blk)),
                  pl.BlockSpec((tq, DSA_WIDTH), lambda b, i: (b * nq + i, OFF_DG // DSA_WIDTH))],
        out_specs=pl.BlockSpec((tq, DSA_WIDTH), lambda b, i: (b * nq + i, 0)),
        out_shape=jax.ShapeDtypeStruct((t, DSA_WIDTH), BF16),
        scratch_shapes=[pltpu.VMEM((seq, tq), jnp.int32),
                        pltpu.VMEM((IDX_HEADS, LANES, tq), BF16),
                        pltpu.VMEM((DSA_KV_HEADS, HEAD_DIM, rep * tq), F32)],
        compiler_params=_cparams("parallel", "arbitrary"),
        name="dsa_attention",
    )(dqt, dk, dvt, iqt, ki, proj, proj)


def _ssd_kernel(xin_ref, z_ref, misc_ref, cw_ref, cb_ref, dtb_ref, alog_ref, dsk_ref, nw_ref,
                e_ref, o_ref, xbuf, xbc, ybuf, state, *, L):
    @pl.when(pl.program_id(1) == 0)
    def _():
        xbuf[0:8, :] = jnp.zeros((8, CONV_CH), F32)
        state[...] = jnp.zeros_like(state)

    xbuf[8:8 + L, :] = xin_ref[...]
    lc = 512
    for ch in range(CONV_CH // lc):
        ls = slice(ch * lc, (ch + 1) * lc)
        acc = cb_ref[:, ls] + cw_ref[3:4, ls] * xbuf[8:8 + L, ls]
        acc = acc + cw_ref[2:3, ls] * xbuf[7:7 + L, ls]
        acc = acc + cw_ref[1:2, ls] * xbuf[6:6 + L, ls]
        acc = acc + cw_ref[0:1, ls] * xbuf[5:5 + L, ls]
        xbc[:, ls] = _silu(acc)
    xbuf[0:8, :] = xbuf[L:L + 8, :]

    dt = _softplus(misc_ref[...] + dtb_ref[...])
    a = -jnp.exp(alog_ref[...])
    acs = _cumsum_rows(dt * a)
    ea = jnp.exp(acs)
    dte = jnp.exp(acs[L - 1:L, :] - acs)
    full = _dot_exact_rhs(jnp.concatenate([dt, ea, dte], axis=0), e_ref[...])
    dt_f = full[0:L]
    ea_f = full[L:2 * L]
    dte_f = full[2 * L:3 * L]
    acs_t = acs.T
    lane = lax.broadcasted_iota(jnp.int32, (L, LANES), 1)
    r = lax.broadcasted_iota(jnp.int32, (L, L), 0)
    c = lax.broadcasted_iota(jnp.int32, (L, L), 1)
    causal = c <= r

    gw = SSD_WIDTH // SSD_GROUPS
    hpg = SSD_HEADS // SSD_GROUPS
    for g in range(SSD_GROUPS):
        gs = slice(g * gw, (g + 1) * gw)
        xd = xbc[:, gs] * dt_f[:, gs]
        xd_b = xd.astype(BF16)
        bg = xbc[:, SSD_WIDTH + g * SSD_STATE:SSD_WIDTH + (g + 1) * SSD_STATE]
        cg = xbc[:, SSD_WIDTH + (SSD_GROUPS + g) * SSD_STATE:
                 SSD_WIDTH + (SSD_GROUPS + g + 1) * SSD_STATE].astype(BF16)
        cbm = lax.dot_general(cg, bg.astype(BF16), (((1,), (1,)), ((), ())),
                              preferred_element_type=F32)
        for pr in range(hpg // 2):
            xpair = xd_b[:, pr * LANES:(pr + 1) * LANES]
            ys = []
            for hh in range(2):
                hl = MISC_DT + g * hpg + 2 * pr + hh
                seg = jnp.where(causal, acs[:, hl:hl + 1] - acs_t[hl:hl + 1, :], NEG)
                mm = (cbm * jnp.exp(seg)).astype(BF16)
                ys.append(jnp.dot(mm, xpair, preferred_element_type=F32))
            ybuf[:, g * gw + pr * LANES:g * gw + (pr + 1) * LANES] = jnp.where(
                lane < SSD_HEAD_DIM, ys[0], ys[1])
        st = state[:, gs]
        yoff = jnp.dot(cg, st.astype(BF16), preferred_element_type=F32)
        ybuf[:, gs] = ybuf[:, gs] + yoff * ea_f[:, gs]
        zz = (xd * dte_f[:, gs]).astype(BF16)
        state[:, gs] = st * ea_f[L - 1:L, gs] + jnp.dot(bg.T.astype(BF16), zz,
                                                        preferred_element_type=F32)
        y = ybuf[:, gs] + dsk_ref[:, gs] * xbc[:, gs]
        y = y * _silu(z_ref[:, gs])
        o_ref[:, gs] = _rms(y, nw_ref[:, gs]).astype(BF16)


def _ssd(proj, conv_w, conv_b, dtb128, alog128, dskip_f, ssd_norm, expand, batch, seq, misc_blk):
    t = proj.shape[0]
    L = SSD_CHUNK
    nc = seq // L
    const = lambda shape: pl.BlockSpec(shape, lambda b, c: (0, 0))
    return pl.pallas_call(
        functools.partial(_ssd_kernel, L=L),
        grid=(batch, nc),
        in_specs=[pl.BlockSpec((L, CONV_CH), lambda b, c: (b * nc + c, 0)),
                  pl.BlockSpec((L, SSD_WIDTH), lambda b, c: (b * nc + c, OFF_SZ // SSD_WIDTH)),
                  pl.BlockSpec((L, LANES), lambda b, c: (b * nc + c, misc_blk)),
                  const((CONV_WIDTH, CONV_CH)), const((1, CONV_CH)), const((1, LANES)),
                  const((1, LANES)), const((1, SSD_WIDTH)), const((1, SSD_WIDTH)),
                  const((LANES, SSD_WIDTH))],
        out_specs=pl.BlockSpec((L, SSD_WIDTH), lambda b, c: (b * nc + c, 0)),
        out_shape=jax.ShapeDtypeStruct((t, SSD_WIDTH), BF16),
        scratch_shapes=[pltpu.VMEM((L + 8, CONV_CH), F32), pltpu.VMEM((L, CONV_CH), F32),
                        pltpu.VMEM((L, SSD_WIDTH), F32), pltpu.VMEM((SSD_STATE, SSD_WIDTH), F32)],
        compiler_params=_cparams("parallel", "arbitrary"),
        name="ssd_scan",
    )(proj, proj, proj, conv_w, conv_b.reshape(1, CONV_CH), dtb128, alog128, dskip_f,
      ssd_norm.reshape(1, SSD_WIDTH), expand)


def _merge_kernel(yf_ref, yd_ref, ys_ref, wf_ref, wd_ref, ws_ref, g0_ref, g1_ref, g2_ref,
                  bm_ref, o_ref):
    a = jnp.dot(yf_ref[...], wf_ref[...], preferred_element_type=F32)
    b = jnp.dot(yd_ref[...], wd_ref[...], preferred_element_type=F32)
    c = jnp.dot(ys_ref[...], ws_ref[...], preferred_element_type=F32)
    m = (jax.nn.sigmoid(g0_ref[...] + bm_ref[0:1, :]) * a
         + jax.nn.sigmoid(g1_ref[...] + bm_ref[1:2, :]) * b
         + jax.nn.sigmoid(g2_ref[...] + bm_ref[2:3, :]) * c)
    o_ref[...] = m.astype(BF16)


def _merge(y_fox, y_dsa, y_ssd, wf, wd, ws, proj, b_merge):
    t = y_fox.shape[0]
    d = wf.shape[1]
    tm = OUT_TM
    tn = OUT_TN
    gb = OFF_MG // tn
    gspec = lambda br: pl.BlockSpec((tm, tn), lambda j, i: (i, gb + br * (d // tn) + j))
    return pl.pallas_call(
        _merge_kernel,
        grid=(d // tn, t // tm),
        in_specs=[pl.BlockSpec((tm, FOX_WIDTH), lambda j, i: (i, 0)),
                  pl.BlockSpec((tm, DSA_WIDTH), lambda j, i: (i, 0)),
                  pl.BlockSpec((tm, SSD_WIDTH), lambda j, i: (i, 0)),
                  pl.BlockSpec((FOX_WIDTH, tn), lambda j, i: (0, j)),
                  pl.BlockSpec((DSA_WIDTH, tn), lambda j, i: (0, j)),
                  pl.BlockSpec((SSD_WIDTH, tn), lambda j, i: (0, j)),
                  gspec(0), gspec(1), gspec(2),
                  pl.BlockSpec((N_BRANCH, tn), lambda j, i: (0, j))],
        out_specs=pl.BlockSpec((tm, tn), lambda j, i: (i, j)),
        out_shape=jax.ShapeDtypeStruct((t, d), BF16),
        compiler_params=_cparams("parallel", "arbitrary"),
        name="branch_merge",
    )(y_fox, y_dsa, y_ssd, wf, wd, ws, proj, proj, proj, b_merge)


def _outproj_kernel(m_ref, w_ref, x_ref, g_ref, o_ref):
    out = jnp.dot(m_ref[...], w_ref[...], preferred_element_type=F32)
    o_ref[...] = x_ref[...] + g_ref[...] * out


def _outproj(merged, w_out, x2, gate, seq):
    t, d = x2.shape
    tm = OUT_TM
    tn = OUT_TN
    return pl.pallas_call(
        _outproj_kernel,
        grid=(d // tn, t // tm),
        in_specs=[pl.BlockSpec((tm, d), lambda j, i: (i, 0)),
                  pl.BlockSpec((d, tn), lambda j, i: (0, j)),
                  pl.BlockSpec((tm, tn), lambda j, i: (i, j)),
                  pl.BlockSpec((None, 1, tn), lambda j, i: (i * tm // seq, 0, j))],
        out_specs=pl.BlockSpec((tm, tn), lambda j, i: (i, j)),
        out_shape=jax.ShapeDtypeStruct((t, d), F32),
        compiler_params=_cparams("parallel", "arbitrary"),
        name="out_proj_residual",
    )(merged, w_out, x2, gate)


def _lane_block(vec, off):
    return jnp.zeros((1, LANES), F32).at[0, off:off + vec.shape[0]].set(vec.astype(F32))


def kernel(x, c, positions, norm_w, w_ada, b_ada, w_in, b_fox_f, fox_q_norm, fox_k_norm, dsa_q_norm, dsa_k_norm, conv_w, conv_b, dt_bias, a_log, d_skip, ssd_norm, b_merge, w_o_fox, w_o_dsa, w_o_ssd, w_out):
    batch, seq, d = x.shape
    depth = w_in.shape[0]
    t = batch * seq
    assert seq % 512 == 0 and d % 512 == 0

    c8 = jnp.zeros((8, d), F32).at[:batch].set(c)
    mod = _ada(c8, w_ada, b_ada)
    tabs = _rope_tables(positions.reshape(t, 1))

    expand = np.zeros((LANES, SSD_WIDTH), np.float32)
    for hh in range(SSD_HEADS):
        expand[MISC_DT + hh, hh * SSD_HEAD_DIM:(hh + 1) * SSD_HEAD_DIM] = 1.0
    expand = jnp.asarray(expand, BF16)

    off_dk = OFF_MG + N_BRANCH * d
    misc_blk = (off_dk + 2 * DSA_KV_WIDTH) // LANES

    h = x.reshape(t, d)
    for l in range(depth):
        shift = mod[l, :batch, 0:d].reshape(batch, 1, d)
        scale = mod[l, :batch, d:2 * d].reshape(batch, 1, d)
        gate = mod[l, :batch, 2 * d:3 * d].reshape(batch, 1, d)
        proj = _inproj(h, norm_w[l], scale, shift, _pack_w_in(w_in[l]), seq)

        fqt, fka, fvt = _fox_prep(proj, _lane_block(b_fox_f[l], MISC_FF), fox_q_norm[l],
                                  fox_k_norm[l], seq, misc_blk)
        y_fox = _fox_attention(fqt, fka, fvt, proj, batch, seq)

        dqt, iqt, dk, dvt, ki = _dsa_prep(proj, tabs, dsa_q_norm[l], dsa_k_norm[l], seq,
                                          off_dk, misc_blk)
        y_dsa = _dsa_attention(dqt, dk, dvt, iqt, ki, proj, batch, seq, misc_blk)

        y_ssd = _ssd(proj, conv_w[l], conv_b[l], _lane_block(dt_bias[l], MISC_DT),
                     _lane_block(a_log[l], MISC_DT),
                     jnp.repeat(d_skip[l].astype(F32), SSD_HEAD_DIM).reshape(1, SSD_WIDTH),
                     ssd_norm[l], expand, batch, seq, misc_blk)

        merged = _merge(y_fox, y_dsa, y_ssd, w_o_fox[l].astype(BF16), w_o_dsa[l].astype(BF16),
                        w_o_ssd[l].astype(BF16), proj, b_merge[l])
        h = _outproj(merged, w_out[l].astype(BF16), h, gate, seq)
    return h.reshape(batch, seq, d)
```
